```python
import math
import jax, jax.numpy as jnp
from jax import lax
import numpy as np

D_MODEL = 1024
BATCH = 8
SEQ = 4096
DEPTH = 4

MLA_HEADS = 8
MLA_Q_LORA = 256
MLA_KV_LORA = 128
MLA_NOPE_DIM = 64
MLA_ROPE_DIM = 32
MLA_V_DIM = 64
ROPE_THETA = 10000.0
Q_BLOCK = 128
SC_DIM = 256
SC_WIDTH = 3
SSD_HEADS = 4
SSD_HEAD_DIM = 64
SSD_GROUPS = 2
SSD_STATE = 128
SSD_CONV_WIDTH = 4
SSD_CHUNK = 128
FFN_DIM = 2816
FFN_CONV_WIDTH = 3
NORM_EPS = 1e-6

MLA_QK_DIM = MLA_NOPE_DIM + MLA_ROPE_DIM
MLA_OUT = MLA_HEADS * MLA_V_DIM
SSD_DIM = SSD_HEADS * SSD_HEAD_DIM
SSD_CONV_DIM = SSD_DIM + 2 * SSD_GROUPS * SSD_STATE
SSD_IN = SSD_DIM + SSD_CONV_DIM + SSD_HEADS
IN_WIDTHS = (MLA_Q_LORA, MLA_KV_LORA, MLA_ROPE_DIM, SC_DIM, SC_DIM, SC_DIM, SSD_IN)
IN_SPLITS = tuple(int(v) for v in np.cumsum(IN_WIDTHS)[:-1])
D_IN = sum(IN_WIDTHS)
D_MIX = MLA_OUT + SC_DIM + SSD_DIM

kernel_name = "hybrid_mla_shortconv_ssd_convffn"


def rms_norm(x, w):
    xf = x.astype(jnp.float32)
    y = xf * lax.rsqrt(jnp.mean(xf * xf, axis=-1, keepdims=True) + NORM_EPS)
    return (y * w.astype(jnp.float32)).astype(x.dtype)


def causal_dwconv(u, w):
    width = w.shape[0]
    s = u.shape[1]
    up = jnp.pad(u, ((0, 0), (width - 1, 0), (0, 0)))
    out = up[:, 0:s] * w[0]
    for i in range(1, width):
        out = out + up[:, i:i + s] * w[i]
    return out


def rope(x, cos, sin):
    x1, x2 = jnp.split(x, 2, axis=-1)
    return jnp.concatenate([x1 * cos - x2 * sin, x2 * cos + x1 * sin], axis=-1).astype(x.dtype)


def mla_mixer(c_q, c_kv, k_rope, cos, sin, q_norm, w_q_up, kv_norm, w_kv_up):
    b, s, _ = c_q.shape
    q = (rms_norm(c_q, q_norm) @ w_q_up).reshape(b, s, MLA_HEADS, MLA_QK_DIM)
    q_nope = q[..., :MLA_NOPE_DIM]
    q_rope = rope(q[..., MLA_NOPE_DIM:], cos[:, :, None, :], sin[:, :, None, :])
    kv = (rms_norm(c_kv, kv_norm) @ w_kv_up).reshape(b, s, MLA_HEADS, MLA_NOPE_DIM + MLA_V_DIM)
    k_nope = kv[..., :MLA_NOPE_DIM]
    v = kv[..., MLA_NOPE_DIM:]
    k_rope = rope(k_rope, cos, sin)
    scale = MLA_QK_DIM ** -0.5
    key_idx = jnp.arange(s)

    def block(i):
        start = i * Q_BLOCK
        qn = lax.dynamic_slice_in_dim(q_nope, start, Q_BLOCK, axis=1)
        qr = lax.dynamic_slice_in_dim(q_rope, start, Q_BLOCK, axis=1)
        sc = jnp.einsum("bqhd,bkhd->bhqk", qn, k_nope) + jnp.einsum("bqhd,bkd->bhqk", qr, k_rope)
        sc = sc.astype(jnp.float32) * scale
        causal = (start + jnp.arange(Q_BLOCK))[:, None] >= key_idx[None, :]
        p = jax.nn.softmax(jnp.where(causal, sc, -jnp.inf), axis=-1)
        return jnp.einsum("bhqk,bkhd->bqhd", p.astype(v.dtype), v)

    out = lax.map(block, jnp.arange(s // Q_BLOCK))
    return out.transpose(1, 0, 2, 3, 4).reshape(b, s, MLA_OUT)


def short_conv_mixer(gate_b, gate_c, h, conv_w):
    return gate_b * causal_dwconv(gate_c * h, conv_w)


def ssd_scan(xdt, a_dt, bh, ch):
    b, s, h, p = xdt.shape
    n = bh.shape[-1]
    L = SSD_CHUNK
    c = s // L
    xdt = xdt.reshape(b, c, L, h, p)
    bh = bh.reshape(b, c, L, h, n)
    ch = ch.reshape(b, c, L, h, n)
    a_cs = jnp.cumsum(a_dt.reshape(b, c, L, h).transpose(0, 3, 1, 2), axis=-1)
    diff = a_cs[..., :, None] - a_cs[..., None, :]
    tri = jnp.tril(jnp.ones((L, L), dtype=bool))
    decay_in = jnp.exp(jnp.where(tri, diff, -jnp.inf))
    scores = jnp.einsum("bclhn,bcshn->bhcls", ch, bh) * decay_in
    y_diag = jnp.einsum("bhcls,bcshp->bclhp", scores, xdt)
    decay_to_end = jnp.exp(a_cs[..., -1:] - a_cs).transpose(0, 2, 3, 1)
    chunk_states = jnp.einsum("bclhn,bclhp->bchpn", bh * decay_to_end[..., None], xdt)
    chunk_decay = jnp.exp(a_cs[..., -1]).transpose(2, 0, 1)

    def step(state, inp):
        st, dec = inp
        return state * dec[..., None, None] + st, state

    init = jnp.zeros((b, h, p, n), chunk_states.dtype)
    _, prev = lax.scan(step, init, (chunk_states.transpose(1, 0, 2, 3, 4), chunk_decay))
    prev = prev.transpose(1, 0, 2, 3, 4)
    decay_from_start = jnp.exp(a_cs).transpose(0, 2, 3, 1)
    y_off = jnp.einsum("bclhn,bchpn->bclhp", ch, prev) * decay_from_start[..., None]
    return (y_diag + y_off).reshape(b, s, h, p)


def ssd_mixer(zxbcdt, conv_w, conv_b, dt_bias, a_log, d_skip, norm_w):
    b, s, _ = zxbcdt.shape
    z = zxbcdt[..., :SSD_DIM]
    xbc = zxbcdt[..., SSD_DIM:SSD_DIM + SSD_CONV_DIM]
    dt = zxbcdt[..., SSD_DIM + SSD_CONV_DIM:]
    xbc = jax.nn.silu(causal_dwconv(xbc, conv_w) + conv_b)
    xs = xbc[..., :SSD_DIM].reshape(b, s, SSD_HEADS, SSD_HEAD_DIM)
    heads_per_group = SSD_HEADS // SSD_GROUPS
    gn = SSD_GROUPS * SSD_STATE
    bm = jnp.repeat(xbc[..., SSD_DIM:SSD_DIM + gn].reshape(b, s, SSD_GROUPS, SSD_STATE), heads_per_group, axis=2)
    cm = jnp.repeat(xbc[..., SSD_DIM + gn:].reshape(b, s, SSD_GROUPS, SSD_STATE), heads_per_group, axis=2)
    dt = jax.nn.softplus(dt.astype(jnp.float32) + dt_bias.astype(jnp.float32))
    a = -jnp.exp(a_log.astype(jnp.float32))
    y = ssd_scan(xs * dt[..., None], dt * a, bm, cm)
    y = y + xs * d_skip[:, None]
    y = y.reshape(b, s, SSD_DIM).astype(zxbcdt.dtype)
    return rms_norm(y * jax.nn.silu(z), norm_w)


def setup_inputs(seed: int = 0) -> dict:
    key = jax.random.key(seed)
    ks = jax.random.split(key, 24)
    L = DEPTH

    def normal(k, shape, scale):
        return scale * jax.random.normal(k, shape, jnp.float32)

    def gain(k, n):
        return 1.0 + normal(k, (L, n), 0.02)

    dt0 = jnp.exp(jax.random.uniform(ks[15], (L, SSD_HEADS), jnp.float32, math.log(1e-3), math.log(1e-1)))
    return {
        "x": normal(ks[0], (BATCH, SEQ, D_MODEL), 1.0),
        "positions": jnp.broadcast_to(jnp.arange(SEQ, dtype=jnp.int32), (BATCH, SEQ)),
        "norm_mix_pre": gain(ks[1], D_MODEL),
        "norm_mix_post": gain(ks[2], D_MODEL),
        "norm_ffn_pre": gain(ks[3], D_MODEL),
        "norm_ffn_post": gain(ks[4], D_MODEL),
        "w_in": normal(ks[5], (L, D_MODEL, D_IN), D_MODEL ** -0.5),
        "mla_q_norm": gain(ks[6], MLA_Q_LORA),
        "mla_w_q_up": normal(ks[7], (L, MLA_Q_LORA, MLA_HEADS * MLA_QK_DIM), MLA_Q_LORA ** -0.5),
        "mla_kv_norm": gain(ks[8], MLA_KV_LORA),
        "mla_w_kv_up": normal(ks[9], (L, MLA_KV_LORA, MLA_HEADS * (MLA_NOPE_DIM + MLA_V_DIM)), MLA_KV_LORA ** -0.5),
        "sc_conv_w": normal(ks[10], (L, SC_WIDTH, SC_DIM), SC_WIDTH ** -0.5),
        "ssd_conv_w": normal(ks[11], (L, SSD_CONV_WIDTH, SSD_CONV_DIM), SSD_CONV_WIDTH ** -0.5),
        "ssd_conv_b": normal(ks[12], (L, SSD_CONV_DIM), 0.02),
        "ssd_dt_bias": dt0 + jnp.log(-jnp.expm1(-dt0)),
        "ssd_a_log": jnp.log(jax.random.uniform(ks[13], (L, SSD_HEADS), jnp.float32, 1.0, 16.0)),
        "ssd_d": 1.0 + normal(ks[14], (L, SSD_HEADS), 0.1),
        "ssd_norm": gain(ks[16], SSD_DIM),
        "w_out": normal(ks[17], (L, D_MIX, D_MODEL), D_MIX ** -0.5),
        "ffn_w_up": normal(ks[18], (L, D_MODEL, 2 * FFN_DIM), D_MODEL ** -0.5),
        "ffn_conv_w": normal(ks[19], (L, FFN_CONV_WIDTH, 2 * FFN_DIM), FFN_CONV_WIDTH ** -0.5),
        "ffn_conv_b": normal(ks[20], (L, 2 * FFN_DIM), 0.02),
        "ffn_w_down": normal(ks[21], (L, FFN_DIM, D_MODEL), FFN_DIM ** -0.5),
    }


def reference(x, positions, norm_mix_pre, norm_mix_post, norm_ffn_pre, norm_ffn_post, w_in,
              mla_q_norm, mla_w_q_up, mla_kv_norm, mla_w_kv_up, sc_conv_w, ssd_conv_w, ssd_conv_b,
              ssd_dt_bias, ssd_a_log, ssd_d, ssd_norm, w_out, ffn_w_up, ffn_conv_w, ffn_conv_b,
              ffn_w_down):
    inv_freq = 1.0 / (ROPE_THETA ** (jnp.arange(0, MLA_ROPE_DIM, 2, dtype=jnp.float32) / MLA_ROPE_DIM))
    ang = positions.astype(jnp.float32)[..., None] * inv_freq
    cos = jnp.cos(ang).astype(x.dtype)
    sin = jnp.sin(ang).astype(x.dtype)
    for l in range(DEPTH):
        h = rms_norm(x, norm_mix_pre[l])
        c_q, c_kv, k_rope, sc_b, sc_c, sc_h, ssd_in = jnp.split(h @ w_in[l], IN_SPLITS, axis=-1)
        y_att = mla_mixer(c_q, c_kv, k_rope, cos, sin, mla_q_norm[l], mla_w_q_up[l], mla_kv_norm[l], mla_w_kv_up[l])
        y_conv = short_conv_mixer(sc_b, sc_c, sc_h, sc_conv_w[l])
        y_ssd = ssd_mixer(ssd_in, ssd_conv_w[l], ssd_conv_b[l], ssd_dt_bias[l], ssd_a_log[l], ssd_d[l], ssd_norm[l])
        mixed = jnp.concatenate([y_att, y_conv, y_ssd], axis=-1) @ w_out[l]
        x = x + rms_norm(mixed, norm_mix_post[l])
        h = rms_norm(x, norm_ffn_pre[l])
        u = causal_dwconv(h @ ffn_w_up[l], ffn_conv_w[l]) + ffn_conv_b[l]
        gate, up = jnp.split(u, 2, axis=-1)
        x = x + rms_norm((jax.nn.silu(gate) * up) @ ffn_w_down[l], norm_ffn_post[l])
    return x
```

```python
import functools
import math

import jax
import jax.numpy as jnp
import numpy as np
from jax import lax
from jax.experimental import pallas as pl
from jax.experimental.pallas import tpu as pltpu

D_MODEL = 1024
MLA_HEADS = 8
MLA_Q_LORA = 256
MLA_KV_LORA = 128
MLA_NOPE_DIM = 64
MLA_ROPE_DIM = 32
MLA_V_DIM = 64
ROPE_THETA = 10000.0
SC_DIM = 256
SC_WIDTH = 3
SSD_HEADS = 4
SSD_HEAD_DIM = 64
SSD_GROUPS = 2
SSD_STATE = 128
SSD_CONV_WIDTH = 4
SSD_CHUNK = 128
FFN_DIM = 2816
FFN_CONV_WIDTH = 3
NORM_EPS = 1e-6

MLA_QK_DIM = MLA_NOPE_DIM + MLA_ROPE_DIM
MLA_OUT = MLA_HEADS * MLA_V_DIM
SSD_DIM = SSD_HEADS * SSD_HEAD_DIM
SSD_BC_DIM = SSD_GROUPS * SSD_STATE
SSD_CONV_DIM = SSD_DIM + 2 * SSD_BC_DIM
SSD_IN = SSD_DIM + SSD_CONV_DIM + SSD_HEADS
IN_WIDTHS = (MLA_Q_LORA, MLA_KV_LORA, MLA_ROPE_DIM, SC_DIM, SC_DIM, SC_DIM, SSD_IN)
D_IN = sum(IN_WIDTHS)

LANES = 128
HEAD_PAD = LANES
HALO = 16

_M_CQ = 0
_M_CKV = _M_CQ + MLA_Q_LORA
_M_KR = _M_CKV + MLA_KV_LORA
_M_SCB = _M_KR + LANES
_M_Z = _M_SCB + SC_DIM
_M_DT = _M_Z + SSD_DIM
_M_END = _M_DT + LANES
_C_SCC = 0
_C_SCH = _C_SCC + SC_DIM
_C_XBC = _C_SCH + SC_DIM
_C_END = _C_XBC + SSD_CONV_DIM

SEQ_TILE_MIX = 512
SEQ_TILE_ATT_Q = 512
SEQ_TILE_ATT_K = 512
SEQ_TILE_FFN = 512
FFN_CHUNK = 1408
VMEM_LIMIT = 56 * 1024 * 1024

_F32 = jnp.float32
_BF16 = jnp.bfloat16
_NT = (((1,), (1,)), ((), ()))
_TN = (((0,), (0,)), ((), ()))


def _rms(x, w):
    return x * lax.rsqrt(jnp.mean(x * x, axis=-1, keepdims=True) + NORM_EPS) * w


def _dot(a, b):
    return jnp.dot(a, b, preferred_element_type=_F32)


def _rope_block(a, c1, c2):
    return a * c1 + pltpu.roll(a, LANES - MLA_ROPE_DIM, 1) * c2


def _mixer_in_kernel(x_ref, c1_ref, c2_ref, gpre_ref, wmain_ref, wconv_ref, qn_ref, wq_ref,
                     kvn_ref, wk_ref, wv_ref, scw_ref, ssw_ref, ssb_ref, dtb_ref, aneg_ref,
                     dskip_ref, snorm_ref,
                     q_ref, k_ref, v_ref, ycs_ref,
                     hbuf, pcbuf, ubuf, ybuf, state_ref, *, ts):
    si = pl.program_id(1)

    @pl.when(si == 0)
    def _():
        hbuf[0:HALO, :] = jnp.zeros((HALO, D_MODEL), _BF16)
        state_ref[...] = jnp.zeros_like(state_ref)

    h = _rms(x_ref[...], gpre_ref[...]).astype(_BF16)
    hbuf[HALO:, :] = h
    pm = _dot(h, wmain_ref[...])
    pcbuf[...] = _dot(hbuf[...], wconv_ref[...])
    hbuf[0:HALO, :] = h[ts - HALO:, :]

    c1 = c1_ref[...]
    c2 = c2_ref[...]
    scale = MLA_QK_DIM ** -0.5

    qn = _rms(pm[:, _M_CQ:_M_CQ + MLA_Q_LORA], qn_ref[...]).astype(_BF16)
    qfull = _dot(qn, wq_ref[...])
    kvn = _rms(pm[:, _M_CKV:_M_CKV + MLA_KV_LORA], kvn_ref[...]).astype(_BF16)
    kfull = _dot(kvn, wk_ref[...])
    v_ref[...] = _dot(kvn, wv_ref[...]).astype(_BF16)
    kr = _rope_block(pm[:, _M_KR:_M_KR + LANES], c1, c2)
    for hd in range(MLA_HEADS):
        sl = slice(hd * HEAD_PAD, (hd + 1) * HEAD_PAD)
        q_ref[:, sl] = (_rope_block(qfull[:, sl], c1, c2) * scale).astype(_BF16)
        k_ref[:, sl] = (kfull[:, sl] + kr).astype(_BF16)

    ubuf[...] = pcbuf[:, _C_SCC:_C_SCC + SC_DIM] * pcbuf[:, _C_SCH:_C_SCH + SC_DIM]
    conv = ubuf[HALO - SC_WIDTH + 1:HALO - SC_WIDTH + 1 + ts, :] * scw_ref[0:1, :]
    for i in range(1, SC_WIDTH):
        o = HALO - SC_WIDTH + 1 + i
        conv = conv + ubuf[o:o + ts, :] * scw_ref[i:i + 1, :]
    ycs_ref[:, 0:SC_DIM] = (pm[:, _M_SCB:_M_SCB + SC_DIM] * conv).astype(_BF16)

    o0 = HALO - SSD_CONV_WIDTH + 1
    xbc = pcbuf[o0:o0 + ts, _C_XBC:_C_END] * ssw_ref[0:1, :]
    for i in range(1, SSD_CONV_WIDTH):
        xbc = xbc + pcbuf[o0 + i:o0 + i + ts, _C_XBC:_C_END] * ssw_ref[i:i + 1, :]
    xbc = xbc + ssb_ref[...]
    xbc = xbc * jax.nn.sigmoid(xbc)
    z = pm[:, _M_Z:_M_Z + SSD_DIM]
    dt_raw = pm[:, _M_DT:_M_DT + LANES] + dtb_ref[...]
    dt = jnp.maximum(dt_raw, 0.0) + jnp.log1p(jnp.exp(-jnp.abs(dt_raw)))
    adt = dt * aneg_ref[...]

    L = SSD_CHUNK
    rows = lax.broadcasted_iota(jnp.int32, (L, L), 0)
    cols = lax.broadcasted_iota(jnp.int32, (L, L), 1)
    tri = rows >= cols
    tri_f = tri.astype(_F32)
    hpg = SSD_HEADS // SSD_GROUPS
    for c in range(ts // L):
        r = slice(c * L, (c + 1) * L)
        xs_c = xbc[r, 0:SSD_DIM]
        dt_c = dt[r, :]
        a_cs = jnp.dot(tri_f, adt[r, :], preferred_element_type=_F32,
                       precision=lax.Precision.HIGHEST)
        a_cs_t = a_cs.T
        for g in range(SSD_GROUPS):
            bg = xbc[r, SSD_DIM + g * SSD_STATE:SSD_DIM + (g + 1) * SSD_STATE]
            cg = xbc[r, SSD_DIM + SSD_BC_DIM + g * SSD_STATE:
                     SSD_DIM + SSD_BC_DIM + (g + 1) * SSD_STATE].astype(_BF16)
            sc = lax.dot_general(cg, bg.astype(_BF16), _NT, preferred_element_type=_F32)
            for hh in range(g * hpg, (g + 1) * hpg):
                hs = slice(hh * SSD_HEAD_DIM, (hh + 1) * SSD_HEAD_DIM)
                col = a_cs[:, hh:hh + 1]
                row = a_cs_t[hh:hh + 1, :]
                a_end = a_cs[L - 1:L, hh:hh + 1]
                dec = jnp.exp(jnp.where(tri, col - row, -jnp.inf))
                xs_h = xs_c[:, hs]
                xdt = (xs_h * dt_c[:, hh:hh + 1]).astype(_BF16)
                y = _dot((sc * dec).astype(_BF16), xdt)
                st = state_ref[hh]
                y = y + _dot(cg, st.astype(_BF16)) * jnp.exp(col)
                bd = (bg * jnp.exp(a_end - col)).astype(_BF16)
                cs = lax.dot_general(bd, xdt, _TN, preferred_element_type=_F32)
                state_ref[hh] = st * jnp.exp(a_end) + cs
                ybuf[r, hs] = y + xs_h * dskip_ref[:, hs]
    yg = ybuf[...] * (z * jax.nn.sigmoid(z))
    ycs_ref[:, SC_DIM:SC_DIM + SSD_DIM] = _rms(yg, snorm_ref[...]).astype(_BF16)


def _mixer_in(x, c1, c2, p, ts):
    b, s, d = x.shape
    grid = (b, s // ts)
    tile = lambda w: pl.BlockSpec((None, ts, w), lambda bi, si: (bi, si, 0))
    full = lambda a: pl.BlockSpec(a.shape, lambda bi, si: (0,) * a.ndim)
    weights = [p["gpre"], p["wmain"], p["wconv"], p["qn"], p["wq"], p["kvn"], p["wk"], p["wv"],
               p["scw"], p["ssw"], p["ssb"], p["dtb"], p["aneg"], p["dskip"], p["snorm"]]
    out_shape = [
        jax.ShapeDtypeStruct((b, s, MLA_HEADS * HEAD_PAD), _BF16),
        jax.ShapeDtypeStruct((b, s, MLA_HEADS * HEAD_PAD), _BF16),
        jax.ShapeDtypeStruct((b, s, MLA_OUT), _BF16),
        jax.ShapeDtypeStruct((b, s, SC_DIM + SSD_DIM), _BF16),
    ]
    return pl.pallas_call(
        functools.partial(_mixer_in_kernel, ts=ts),
        grid=grid,
        in_specs=[tile(d), tile(LANES), tile(LANES)] + [full(w) for w in weights],
        out_specs=[tile(MLA_HEADS * HEAD_PAD), tile(MLA_HEADS * HEAD_PAD), tile(MLA_OUT),
                   tile(SC_DIM + SSD_DIM)],
        out_shape=out_shape,
        scratch_shapes=[
            pltpu.VMEM((ts + HALO, D_MODEL), _BF16),
            pltpu.VMEM((ts + HALO, _C_END), _F32),
            pltpu.VMEM((ts + HALO, SC_DIM), _F32),
            pltpu.VMEM((ts, SSD_DIM), _F32),
            pltpu.VMEM((SSD_HEADS, SSD_STATE, SSD_HEAD_DIM), _F32),
        ],
        compiler_params=pltpu.CompilerParams(
            dimension_semantics=("arbitrary", "arbitrary"), vmem_limit_bytes=VMEM_LIMIT),
        name="mixer_in",
    )(x, c1, c2, *weights)


def _attention_kernel(q_ref, k_ref, v_ref, o_ref, *, tq, tk):
    qi = pl.program_id(1)
    nkv = (qi + 1) * (tq // tk)
    row = qi * tq + lax.broadcasted_iota(jnp.int32, (tq, tk), 0)
    col0 = lax.broadcasted_iota(jnp.int32, (tq, tk), 1)
    for hd in range(MLA_HEADS):
        q = q_ref[:, hd * HEAD_PAD:(hd + 1) * HEAD_PAD]

        def body(j, carry, hd=hd, q=q):
            m, l, acc = carry
            start = pl.multiple_of(j * tk, tk)
            ks = k_ref[pl.ds(start, tk), hd * HEAD_PAD:(hd + 1) * HEAD_PAD]
            vs = v_ref[pl.ds(start, tk), hd * MLA_V_DIM:(hd + 1) * MLA_V_DIM]
            s = lax.dot_general(q, ks, _NT, preferred_element_type=_F32)
            s = jnp.where(row >= col0 + start, s, -jnp.inf)
            m_new = jnp.maximum(m, jnp.max(s, axis=-1, keepdims=True))
            alpha = jnp.exp(m - m_new)
            pexp = jnp.exp(s - m_new)
            l = alpha * l + jnp.sum(pexp, axis=-1, keepdims=True)
            acc = alpha * acc + _dot(pexp.astype(_BF16), vs)
            return m_new, l, acc

        init = (jnp.full((tq, 1), -jnp.inf, _F32), jnp.zeros((tq, 1), _F32),
                jnp.zeros((tq, MLA_V_DIM), _F32))
        _, l, acc = lax.fori_loop(0, nkv, body, init)
        o_ref[:, hd * MLA_V_DIM:(hd + 1) * MLA_V_DIM] = (acc / l).astype(_BF16)


def _attention(q, k, v, tq, tk):
    b, s, _ = q.shape
    return pl.pallas_call(
        functools.partial(_attention_kernel, tq=tq, tk=tk),
        grid=(b, s // tq),
        in_specs=[
            pl.BlockSpec((None, tq, MLA_HEADS * HEAD_PAD), lambda bi, qi: (bi, qi, 0)),
            pl.BlockSpec((None, s, MLA_HEADS * HEAD_PAD), lambda bi, qi: (bi, 0, 0)),
            pl.BlockSpec((None, s, MLA_OUT), lambda bi, qi: (bi, 0, 0)),
        ],
        out_specs=pl.BlockSpec((None, tq, MLA_OUT), lambda bi, qi: (bi, qi, 0)),
        out_shape=jax.ShapeDtypeStruct((b, s, MLA_OUT), _BF16),
        compiler_params=pltpu.CompilerParams(
            dimension_semantics=("arbitrary", "arbitrary"), vmem_limit_bytes=VMEM_LIMIT),
        name="attention",
    )(q, k, v)


def _out_ffn_kernel(x_ref, att_ref, ycs_ref, woa_ref, woc_ref, gpost_ref, gfpre_ref, wg_ref,
                    wu_ref, cwg_ref, cwu_ref, cbg_ref, cbu_ref, wd_ref, gfpost_ref,
                    o_ref, hbuf, gbuf, ubuf, *, tm, fc):
    si = pl.program_id(1)

    @pl.when(si == 0)
    def _():
        hbuf[0:HALO, :] = jnp.zeros((HALO, D_MODEL), _BF16)

    mixed = _dot(att_ref[...], woa_ref[...]) + _dot(ycs_ref[...], woc_ref[...])
    x1 = x_ref[...] + _rms(mixed, gpost_ref[...])
    h = _rms(x1, gfpre_ref[...]).astype(_BF16)
    hbuf[HALO:, :] = h
    acc = jnp.zeros((tm, D_MODEL), _F32)
    o0 = HALO - FFN_CONV_WIDTH + 1
    for c in range(FFN_DIM // fc):
        cs = slice(c * fc, (c + 1) * fc)
        gbuf[...] = _dot(hbuf[...], wg_ref[:, cs])
        ubuf[...] = _dot(hbuf[...], wu_ref[:, cs])
        g = gbuf[o0:o0 + tm, :] * cwg_ref[0:1, cs]
        u = ubuf[o0:o0 + tm, :] * cwu_ref[0:1, cs]
        for i in range(1, FFN_CONV_WIDTH):
            g = g + gbuf[o0 + i:o0 + i + tm, :] * cwg_ref[i:i + 1, cs]
            u = u + ubuf[o0 + i:o0 + i + tm, :] * cwu_ref[i:i + 1, cs]
        g = g + cbg_ref[:, cs]
        u = u + cbu_ref[:, cs]
        act = (g * jax.nn.sigmoid(g) * u).astype(_BF16)
        acc = acc + _dot(act, wd_ref[cs, :])
    hbuf[0:HALO, :] = h[tm - HALO:, :]
    o_ref[...] = x1 + _rms(acc, gfpost_ref[...])


def _out_ffn(x, att, ycs, p, tm, fc):
    b, s, d = x.shape
    tile = lambda w: pl.BlockSpec((None, tm, w), lambda bi, si: (bi, si, 0))
    full = lambda a: pl.BlockSpec(a.shape, lambda bi, si: (0,) * a.ndim,
                                  pipeline_mode=pl.Buffered(1))
    weights = [p["woa"], p["woc"], p["gpost"], p["gfpre"], p["wg"], p["wu"], p["cwg"], p["cwu"],
               p["cbg"], p["cbu"], p["wd"], p["gfpost"]]
    return pl.pallas_call(
        functools.partial(_out_ffn_kernel, tm=tm, fc=fc),
        grid=(b, s // tm),
        in_specs=[tile(d), tile(MLA_OUT), tile(SC_DIM + SSD_DIM)] + [full(w) for w in weights],
        out_specs=tile(d),
        out_shape=jax.ShapeDtypeStruct((b, s, d), _F32),
        scratch_shapes=[
            pltpu.VMEM((tm + HALO, D_MODEL), _BF16),
            pltpu.VMEM((tm + HALO, fc), _F32),
            pltpu.VMEM((tm + HALO, fc), _F32),
        ],
        compiler_params=pltpu.CompilerParams(
            dimension_semantics=("arbitrary", "arbitrary"), vmem_limit_bytes=VMEM_LIMIT),
        name="out_ffn",
    )(x, att, ycs, *weights)


def _swap_halves(w):
    half = w.shape[-1] // 2
    return jnp.concatenate([w[..., half:], w[..., :half]], axis=-1)


def _pack_layer(l, norm_mix_pre, norm_mix_post, norm_ffn_pre, norm_ffn_post, w_in, mla_q_norm,
                mla_w_q_up, mla_kv_norm, mla_w_kv_up, sc_conv_w, ssd_conv_w, ssd_conv_b,
                ssd_dt_bias, ssd_a_log, ssd_d, ssd_norm, w_out, ffn_w_up, ffn_conv_w, ffn_conv_b,
                ffn_w_down):
    row = lambda a: a[l].reshape(1, -1).astype(_F32)
    splits = np.cumsum(IN_WIDTHS)[:-1]
    w_cq, w_ckv, w_kr, w_scb, w_scc, w_sch, w_ssd = jnp.split(w_in[l], splits, axis=-1)
    w_z = w_ssd[:, :SSD_DIM]
    w_xbc = w_ssd[:, SSD_DIM:SSD_DIM + SSD_CONV_DIM]
    w_dt = w_ssd[:, SSD_DIM + SSD_CONV_DIM:]
    zeros = lambda n: jnp.zeros((D_MODEL, n), w_in.dtype)
    wmain = jnp.concatenate(
        [w_cq, w_ckv, zeros(MLA_NOPE_DIM), w_kr, _swap_halves(w_kr), w_scb, w_z, w_dt,
         zeros(LANES - SSD_HEADS)], axis=-1)
    wconv = jnp.concatenate([w_scc, w_sch, w_xbc], axis=-1)

    wq3 = mla_w_q_up[l].reshape(MLA_Q_LORA, MLA_HEADS, MLA_QK_DIM)
    wq_rope = wq3[..., MLA_NOPE_DIM:]
    wq = jnp.concatenate([wq3, _swap_halves(wq_rope)], axis=-1).reshape(MLA_Q_LORA, -1)
    wkv3 = mla_w_kv_up[l].reshape(MLA_KV_LORA, MLA_HEADS, MLA_NOPE_DIM + MLA_V_DIM)
    wk = jnp.concatenate([wkv3[..., :MLA_NOPE_DIM], jnp.zeros_like(wkv3[..., :MLA_NOPE_DIM])],
                         axis=-1).reshape(MLA_KV_LORA, -1)
    wv = wkv3[..., MLA_NOPE_DIM:].reshape(MLA_KV_LORA, -1)

    pad_heads = lambda a: jnp.concatenate(
        [a[l].astype(_F32), jnp.zeros((LANES - SSD_HEADS,), _F32)]).reshape(1, LANES)
    return {
        "gpre": row(norm_mix_pre), "wmain": wmain.astype(_BF16), "wconv": wconv.astype(_BF16),
        "qn": row(mla_q_norm), "wq": wq.astype(_BF16), "kvn": row(mla_kv_norm),
        "wk": wk.astype(_BF16), "wv": wv.astype(_BF16),
        "scw": sc_conv_w[l].astype(_F32), "ssw": ssd_conv_w[l].astype(_F32),
        "ssb": row(ssd_conv_b), "dtb": pad_heads(ssd_dt_bias),
        "aneg": -jnp.exp(pad_heads(ssd_a_log)),
        "dskip": jnp.repeat(ssd_d[l].astype(_F32), SSD_HEAD_DIM).reshape(1, SSD_DIM),
        "snorm": row(ssd_norm),
        "woa": w_out[l, :MLA_OUT].astype(_BF16), "woc": w_out[l, MLA_OUT:].astype(_BF16),
        "gpost": row(norm_mix_post), "gfpre": row(norm_ffn_pre),
        "wg": ffn_w_up[l, :, :FFN_DIM].astype(_BF16), "wu": ffn_w_up[l, :, FFN_DIM:].astype(_BF16),
        "cwg": ffn_conv_w[l, :, :FFN_DIM].astype(_F32), "cwu": ffn_conv_w[l, :, FFN_DIM:].astype(_F32),
        "cbg": ffn_conv_b[l, :FFN_DIM].reshape(1, -1).astype(_F32),
        "cbu": ffn_conv_b[l, FFN_DIM:].reshape(1, -1).astype(_F32),
        "wd": ffn_w_down[l].astype(_BF16), "gfpost": row(norm_ffn_post),
    }


def _rope_tables(positions):
    inv_freq = 1.0 / (ROPE_THETA ** (jnp.arange(0, MLA_ROPE_DIM, 2, dtype=_F32) / MLA_ROPE_DIM))
    ang = positions.astype(_F32)[..., None] * inv_freq
    cos = jnp.cos(ang)
    sin = jnp.sin(ang)
    lead = positions.shape + (MLA_NOPE_DIM,)
    tail = positions.shape + (LANES - MLA_NOPE_DIM - MLA_ROPE_DIM,)
    c1 = jnp.concatenate([jnp.ones(lead, _F32), cos, cos, jnp.zeros(tail, _F32)], axis=-1)
    c2 = jnp.concatenate([jnp.zeros(lead, _F32), -sin, sin, jnp.zeros(tail, _F32)], axis=-1)
    return c1, c2


def kernel(x, positions, norm_mix_pre, norm_mix_post, norm_ffn_pre, norm_ffn_post, w_in, mla_q_norm, mla_w_q_up, mla_kv_norm, mla_w_kv_up, sc_conv_w, ssd_conv_w, ssd_conv_b, ssd_dt_bias, ssd_a_log, ssd_d, ssd_norm, w_out, ffn_w_up, ffn_conv_w, ffn_conv_b, ffn_w_down):
    depth = w_in.shape[0]
    s = x.shape[1]
    ts = min(SEQ_TILE_MIX, s)
    tq = min(SEQ_TILE_ATT_Q, s)
    tk = min(SEQ_TILE_ATT_K, tq)
    tm = min(SEQ_TILE_FFN, s)
    assert s % ts == 0 and s % tq == 0 and tq % tk == 0 and s % tm == 0 and ts % SSD_CHUNK == 0
    c1, c2 = _rope_tables(positions)
    for l in range(depth):
        p = _pack_layer(l, norm_mix_pre, norm_mix_post, norm_ffn_pre, norm_ffn_post, w_in,
                        mla_q_norm, mla_w_q_up, mla_kv_norm, mla_w_kv_up, sc_conv_w, ssd_conv_w,
                        ssd_conv_b, ssd_dt_bias, ssd_a_log, ssd_d, ssd_norm, w_out, ffn_w_up,
                        ffn_conv_w, ffn_conv_b, ffn_w_down)
        q, k, v, ycs = _mixer_in(x, c1, c2, p, ts)
        att = _attention(q, k, v, tq, tk)
        x = _out_ffn(x, att, ycs, p, tm, FFN_CHUNK)
    return x
```

```python
import functools
import math

import jax
import jax.numpy as jnp
import numpy as np
from jax import lax
from jax.experimental import pallas as pl
from jax.experimental.pallas import tpu as pltpu

D_MODEL = 1024
MLA_HEADS = 8
MLA_Q_LORA = 256
MLA_KV_LORA = 128
MLA_NOPE_DIM = 64
MLA_ROPE_DIM = 32
MLA_V_DIM = 64
ROPE_THETA = 10000.0
SC_DIM = 256
SC_WIDTH = 3
SSD_HEADS = 4
SSD_HEAD_DIM = 64
SSD_GROUPS = 2
SSD_STATE = 128
SSD_CONV_WIDTH = 4
SSD_CHUNK = 128
FFN_DIM = 2816
FFN_CONV_WIDTH = 3
NORM_EPS = 1e-6

MLA_QK_DIM = MLA_NOPE_DIM + MLA_ROPE_DIM
MLA_OUT = MLA_HEADS * MLA_V_DIM
SSD_DIM = SSD_HEADS * SSD_HEAD_DIM
SSD_BC_DIM = SSD_GROUPS * SSD_STATE
SSD_CONV_DIM = SSD_DIM + 2 * SSD_BC_DIM
SSD_IN = SSD_DIM + SSD_CONV_DIM + SSD_HEADS
IN_WIDTHS = (MLA_Q_LORA, MLA_KV_LORA, MLA_ROPE_DIM, SC_DIM, SC_DIM, SC_DIM, SSD_IN)
D_IN = sum(IN_WIDTHS)

LANES = 128
HEAD_PAD = LANES
HALO = 16

_M_CQ = 0
_M_CKV = _M_CQ + MLA_Q_LORA
_M_KR = _M_CKV + MLA_KV_LORA
_M_SCB = _M_KR + LANES
_M_Z = _M_SCB + SC_DIM
_M_DT = _M_Z + SSD_DIM
_M_END = _M_DT + LANES
_C_SCC = 0
_C_SCH = _C_SCC + SC_DIM
_C_XBC = _C_SCH + SC_DIM
_C_END = _C_XBC + SSD_CONV_DIM

SEQ_TILE_MIX = 512
SEQ_TILE_ATT_Q = 512
SEQ_TILE_ATT_K = 512
ATT_SUM_ROWS = 16
SEQ_TILE_FFN = 512
FFN_CHUNK = 1408
VMEM_LIMIT = 56 * 1024 * 1024

_F32 = jnp.float32
_BF16 = jnp.bfloat16
_NT = (((1,), (1,)), ((), ()))
_TN = (((0,), (0,)), ((), ()))


def _rms(x, w):
    return x * lax.rsqrt(jnp.mean(x * x, axis=-1, keepdims=True) + NORM_EPS) * w


def _dot(a, b):
    return jnp.dot(a, b, preferred_element_type=_F32)


def _rope_block(a, c1, c2):
    return a * c1 + pltpu.roll(a, LANES - MLA_ROPE_DIM, 1) * c2


def _mixer_in_kernel(x_ref, c1_ref, c2_ref, gpre_ref, wmain_ref, wconv_ref, qn_ref, wq_ref,
                     kvn_ref, wk_ref, wv_ref, scw_ref, ssw_ref, ssb_ref, dtb_ref, aneg_ref,
                     dskip_ref, snorm_ref,
                     qt_ref, k_ref, vt_ref, ycs_ref,
                     hbuf, pcbuf, ubuf, ybuf, state_ref, *, ts):
    si = pl.program_id(1)

    @pl.when(si == 0)
    def _():
        hbuf[0:HALO, :] = jnp.zeros((HALO, D_MODEL), _BF16)
        state_ref[...] = jnp.zeros_like(state_ref)

    h = _rms(x_ref[...], gpre_ref[...]).astype(_BF16)
    hbuf[HALO:, :] = h
    pm = _dot(h, wmain_ref[...])
    pcbuf[...] = _dot(hbuf[...], wconv_ref[...])
    hbuf[0:HALO, :] = h[ts - HALO:, :]

    c1 = c1_ref[...]
    c2 = c2_ref[...]
    scale = MLA_QK_DIM ** -0.5 * math.log2(math.e)

    qn = _rms(pm[:, _M_CQ:_M_CQ + MLA_Q_LORA], qn_ref[...]).astype(_BF16)
    qfull = _dot(qn, wq_ref[...])
    kvn = _rms(pm[:, _M_CKV:_M_CKV + MLA_KV_LORA], kvn_ref[...]).astype(_BF16)
    kfull = _dot(kvn, wk_ref[...])
    vt_ref[...] = _dot(kvn, wv_ref[...]).T.astype(_BF16)
    kr = _rope_block(pm[:, _M_KR:_M_KR + LANES], c1, c2)
    for hd in range(MLA_HEADS):
        sl = slice(hd * HEAD_PAD, (hd + 1) * HEAD_PAD)
        qt_ref[sl, :] = (_rope_block(qfull[:, sl], c1, c2) * scale).T.astype(_BF16)
        k_ref[:, sl] = (kfull[:, sl] + kr).astype(_BF16)

    ubuf[...] = pcbuf[:, _C_SCC:_C_SCC + SC_DIM] * pcbuf[:, _C_SCH:_C_SCH + SC_DIM]
    conv = ubuf[HALO - SC_WIDTH + 1:HALO - SC_WIDTH + 1 + ts, :] * scw_ref[0:1, :]
    for i in range(1, SC_WIDTH):
        o = HALO - SC_WIDTH + 1 + i
        conv = conv + ubuf[o:o + ts, :] * scw_ref[i:i + 1, :]
    ycs_ref[:, 0:SC_DIM] = (pm[:, _M_SCB:_M_SCB + SC_DIM] * conv).astype(_BF16)

    o0 = HALO - SSD_CONV_WIDTH + 1
    xbc = pcbuf[o0:o0 + ts, _C_XBC:_C_END] * ssw_ref[0:1, :]
    for i in range(1, SSD_CONV_WIDTH):
        xbc = xbc + pcbuf[o0 + i:o0 + i + ts, _C_XBC:_C_END] * ssw_ref[i:i + 1, :]
    xbc = xbc + ssb_ref[...]
    xbc = xbc * jax.nn.sigmoid(xbc)
    z = pm[:, _M_Z:_M_Z + SSD_DIM]
    dt_raw = pm[:, _M_DT:_M_DT + LANES] + dtb_ref[...]
    dt = jnp.maximum(dt_raw, 0.0) + jnp.log1p(jnp.exp(-jnp.abs(dt_raw)))
    adt = dt * aneg_ref[...]

    L = SSD_CHUNK
    rows = lax.broadcasted_iota(jnp.int32, (L, L), 0)
    cols = lax.broadcasted_iota(jnp.int32, (L, L), 1)
    tri = rows >= cols
    tri_f = tri.astype(_F32)
    hpg = SSD_HEADS // SSD_GROUPS
    for c in range(ts // L):
        r = slice(c * L, (c + 1) * L)
        xs_c = xbc[r, 0:SSD_DIM]
        dt_c = dt[r, :]
        a_cs = jnp.dot(tri_f, adt[r, :], preferred_element_type=_F32,
                       precision=lax.Precision.HIGHEST)
        a_cs_t = a_cs.T
        for g in range(SSD_GROUPS):
            bg = xbc[r, SSD_DIM + g * SSD_STATE:SSD_DIM + (g + 1) * SSD_STATE]
            cg = xbc[r, SSD_DIM + SSD_BC_DIM + g * SSD_STATE:
                     SSD_DIM + SSD_BC_DIM + (g + 1) * SSD_STATE].astype(_BF16)
            sc = lax.dot_general(cg, bg.astype(_BF16), _NT, preferred_element_type=_F32)
            for hh in range(g * hpg, (g + 1) * hpg):
                hs = slice(hh * SSD_HEAD_DIM, (hh + 1) * SSD_HEAD_DIM)
                col = a_cs[:, hh:hh + 1]
                row = a_cs_t[hh:hh + 1, :]
                a_end = a_cs[L - 1:L, hh:hh + 1]
                dec = jnp.exp(jnp.where(tri, col - row, -jnp.inf))
                xs_h = xs_c[:, hs]
                xdt = (xs_h * dt_c[:, hh:hh + 1]).astype(_BF16)
                y = _dot((sc * dec).astype(_BF16), xdt)
                st = state_ref[hh]
                y = y + _dot(cg, st.astype(_BF16)) * jnp.exp(col)
                bd = (bg * jnp.exp(a_end - col)).astype(_BF16)
                cs = lax.dot_general(bd, xdt, _TN, preferred_element_type=_F32)
                state_ref[hh] = st * jnp.exp(a_end) + cs
                ybuf[r, hs] = y + xs_h * dskip_ref[:, hs]
    yg = ybuf[...] * (z * jax.nn.sigmoid(z))
    ycs_ref[:, SC_DIM:SC_DIM + SSD_DIM] = _rms(yg, snorm_ref[...]).astype(_BF16)


def _mixer_in(x, c1, c2, p, ts):
    b, s, d = x.shape
    grid = (b, s // ts)
    tile = lambda w: pl.BlockSpec((None, ts, w), lambda bi, si: (bi, si, 0))
    tile_t = lambda w: pl.BlockSpec((None, w, ts), lambda bi, si: (bi, 0, si))
    full = lambda a: pl.BlockSpec(a.shape, lambda bi, si: (0,) * a.ndim)
    weights = [p["gpre"], p["wmain"], p["wconv"], p["qn"], p["wq"], p["kvn"], p["wk"], p["wv"],
               p["scw"], p["ssw"], p["ssb"], p["dtb"], p["aneg"], p["dskip"], p["snorm"]]
    out_shape = [
        jax.ShapeDtypeStruct((b, MLA_HEADS * HEAD_PAD, s), _BF16),
        jax.ShapeDtypeStruct((b, s, MLA_HEADS * HEAD_PAD), _BF16),
        jax.ShapeDtypeStruct((b, MLA_OUT, s), _BF16),
        jax.ShapeDtypeStruct((b, s, SC_DIM + SSD_DIM), _BF16),
    ]
    return pl.pallas_call(
        functools.partial(_mixer_in_kernel, ts=ts),
        grid=grid,
        in_specs=[tile(d), tile(LANES), tile(LANES)] + [full(w) for w in weights],
        out_specs=[tile_t(MLA_HEADS * HEAD_PAD), tile(MLA_HEADS * HEAD_PAD), tile_t(MLA_OUT),
                   tile(SC_DIM + SSD_DIM)],
        out_shape=out_shape,
        scratch_shapes=[
            pltpu.VMEM((ts + HALO, D_MODEL), _BF16),
            pltpu.VMEM((ts + HALO, _C_END), _F32),
            pltpu.VMEM((ts + HALO, SC_DIM), _F32),
            pltpu.VMEM((ts, SSD_DIM), _F32),
            pltpu.VMEM((SSD_HEADS, SSD_STATE, SSD_HEAD_DIM), _F32),
        ],
        compiler_params=pltpu.CompilerParams(
            dimension_semantics=("arbitrary", "arbitrary"), vmem_limit_bytes=VMEM_LIMIT),
        name="mixer_in",
    )(x, c1, c2, *weights)


def _attention_kernel(qt_ref, k_ref, vt_ref, o_ref, sa_ref, sb_ref, *, tq, tk):
    qi = pl.program_id(1)
    causal = (lax.broadcasted_iota(jnp.int32, (tk, tq), 0)
              <= lax.broadcasted_iota(jnp.int32, (tk, tq), 1))
    ones = jnp.ones((ATT_SUM_ROWS, tk), _BF16)
    acc_rows = MLA_V_DIM + ATT_SUM_ROWS

    def scores(sref, hd, j):
        start = pl.multiple_of(j * tk, tk)
        ks = k_ref[pl.ds(start, tk), hd * HEAD_PAD:(hd + 1) * HEAD_PAD]
        sref[...] = _dot(ks, qt_ref[hd * HEAD_PAD:(hd + 1) * HEAD_PAD, :])

    def softmax_pv(sref, hd, j, m, acc, diagonal):
        st = sref[...]
        if diagonal:
            st = jnp.where(causal, st, -jnp.inf)
        m_new = jnp.maximum(m, jnp.max(st, axis=0, keepdims=True))
        alpha = jnp.exp2(m - m_new)
        pt = jnp.exp2(st - m_new).astype(_BF16)
        start = pl.multiple_of(j * tk, tk)
        vt = vt_ref[hd * MLA_V_DIM:(hd + 1) * MLA_V_DIM, pl.ds(start, tk)]
        lhs = jnp.concatenate([vt, ones], axis=0)
        return m_new, alpha * acc + _dot(lhs, pt)

    def finish(hd, acc):
        o = acc[0:MLA_V_DIM, :] / acc[MLA_V_DIM:MLA_V_DIM + 1, :]
        o_ref[:, hd * MLA_V_DIM:(hd + 1) * MLA_V_DIM] = o.T.astype(_BF16)

    for ha in range(0, MLA_HEADS, 2):
        hb = ha + 1
        scores(sa_ref, ha, 0)

        def body(j, carry, ha=ha, hb=hb):
            (ma, acca), (mb, accb) = carry
            scores(sb_ref, hb, j)
            ma, acca = softmax_pv(sa_ref, ha, j, ma, acca, False)
            scores(sa_ref, ha, j + 1)
            mb, accb = softmax_pv(sb_ref, hb, j, mb, accb, False)
            return (ma, acca), (mb, accb)

        init = (jnp.full((1, tq), -jnp.inf, _F32), jnp.zeros((acc_rows, tq), _F32))
        (ma, acca), (mb, accb) = lax.fori_loop(0, qi, body, (init, init))
        scores(sb_ref, hb, qi)
        finish(ha, softmax_pv(sa_ref, ha, qi, ma, acca, True)[1])
        finish(hb, softmax_pv(sb_ref, hb, qi, mb, accb, True)[1])


def _attention(qt, k, vt, tq, tk):
    b, s, _ = k.shape
    assert tq == tk
    return pl.pallas_call(
        functools.partial(_attention_kernel, tq=tq, tk=tk),
        grid=(b, s // tq),
        in_specs=[
            pl.BlockSpec((None, MLA_HEADS * HEAD_PAD, tq), lambda bi, qi: (bi, 0, qi)),
            pl.BlockSpec((None, s, MLA_HEADS * HEAD_PAD), lambda bi, qi: (bi, 0, 0)),
            pl.BlockSpec((None, MLA_OUT, s), lambda bi, qi: (bi, 0, 0)),
        ],
        out_specs=pl.BlockSpec((None, tq, MLA_OUT), lambda bi, qi: (bi, qi, 0)),
        out_shape=jax.ShapeDtypeStruct((b, s, MLA_OUT), _BF16),
        scratch_shapes=[pltpu.VMEM((tk, tq), _F32), pltpu.VMEM((tk, tq), _F32)],
        compiler_params=pltpu.CompilerParams(
            dimension_semantics=("arbitrary", "arbitrary"), vmem_limit_bytes=VMEM_LIMIT),
        name="attention",
    )(qt, k, vt)


def _out_ffn_kernel(x_ref, att_ref, ycs_ref, woa_ref, woc_ref, gpost_ref, gfpre_ref, wg_ref,
                    wu_ref, cwg_ref, cwu_ref, cbg_ref, cbu_ref, wd_ref, gfpost_ref,
                    o_ref, hbuf, gbuf, ubuf, *, tm, fc):
    si = pl.program_id(1)

    @pl.when(si == 0)
    def _():
        hbuf[0:HALO, :] = jnp.zeros((HALO, D_MODEL), _BF16)

    mixed = _dot(att_ref[...], woa_ref[...]) + _dot(ycs_ref[...], woc_ref[...])
    x1 = x_ref[...] + _rms(mixed, gpost_ref[...])
    h = _rms(x1, gfpre_ref[...]).astype(_BF16)
    hbuf[HALO:, :] = h
    acc = jnp.zeros((tm, D_MODEL), _F32)
    o0 = HALO - FFN_CONV_WIDTH + 1
    for c in range(FFN_DIM // fc):
        cs = slice(c * fc, (c + 1) * fc)
        gbuf[...] = _dot(hbuf[...], wg_ref[:, cs])
        ubuf[...] = _dot(hbuf[...], wu_ref[:, cs])
        g = gbuf[o0:o0 + tm, :] * cwg_ref[0:1, cs]
        u = ubuf[o0:o0 + tm, :] * cwu_ref[0:1, cs]
        for i in range(1, FFN_CONV_WIDTH):
            g = g + gbuf[o0 + i:o0 + i + tm, :] * cwg_ref[i:i + 1, cs]
            u = u + ubuf[o0 + i:o0 + i + tm, :] * cwu_ref[i:i + 1, cs]
        g = g + cbg_ref[:, cs]
        u = u + cbu_ref[:, cs]
        act = (g * jax.nn.sigmoid(g) * u).astype(_BF16)
        acc = acc + _dot(act, wd_ref[cs, :])
    hbuf[0:HALO, :] = h[tm - HALO:, :]
    o_ref[...] = x1 + _rms(acc, gfpost_ref[...])


def _out_ffn(x, att, ycs, p, tm, fc):
    b, s, d = x.shape
    tile = lambda w: pl.BlockSpec((None, tm, w), lambda bi, si: (bi, si, 0))
    full = lambda a: pl.BlockSpec(a.shape, lambda bi, si: (0,) * a.ndim,
                                  pipeline_mode=pl.Buffered(1))
    weights = [p["woa"], p["woc"], p["gpost"], p["gfpre"], p["wg"], p["wu"], p["cwg"], p["cwu"],
               p["cbg"], p["cbu"], p["wd"], p["gfpost"]]
    return pl.pallas_call(
        functools.partial(_out_ffn_kernel, tm=tm, fc=fc),
        grid=(b, s // tm),
        in_specs=[tile(d), tile(MLA_OUT), tile(SC_DIM + SSD_DIM)] + [full(w) for w in weights],
        out_specs=tile(d),
        out_shape=jax.ShapeDtypeStruct((b, s, d), _F32),
        scratch_shapes=[
            pltpu.VMEM((tm + HALO, D_MODEL), _BF16),
            pltpu.VMEM((tm + HALO, fc), _F32),
            pltpu.VMEM((tm + HALO, fc), _F32),
        ],
        compiler_params=pltpu.CompilerParams(
            dimension_semantics=("arbitrary", "arbitrary"), vmem_limit_bytes=VMEM_LIMIT),
        name="out_ffn",
    )(x, att, ycs, *weights)


def _swap_halves(w):
    half = w.shape[-1] // 2
    return jnp.concatenate([w[..., half:], w[..., :half]], axis=-1)


def _pack_layer(l, norm_mix_pre, norm_mix_post, norm_ffn_pre, norm_ffn_post, w_in, mla_q_norm,
                mla_w_q_up, mla_kv_norm, mla_w_kv_up, sc_conv_w, ssd_conv_w, ssd_conv_b,
                ssd_dt_bias, ssd_a_log, ssd_d, ssd_norm, w_out, ffn_w_up, ffn_conv_w, ffn_conv_b,
                ffn_w_down):
    row = lambda a: a[l].reshape(1, -1).astype(_F32)
    splits = np.cumsum(IN_WIDTHS)[:-1]
    w_cq, w_ckv, w_kr, w_scb, w_scc, w_sch, w_ssd = jnp.split(w_in[l], splits, axis=-1)
    w_z = w_ssd[:, :SSD_DIM]
    w_xbc = w_ssd[:, SSD_DIM:SSD_DIM + SSD_CONV_DIM]
    w_dt = w_ssd[:, SSD_DIM + SSD_CONV_DIM:]
    zeros = lambda n: jnp.zeros((D_MODEL, n), w_in.dtype)
    wmain = jnp.concatenate(
        [w_cq, w_ckv, zeros(MLA_NOPE_DIM), w_kr, _swap_halves(w_kr), w_scb, w_z, w_dt,
         zeros(LANES - SSD_HEADS)], axis=-1)
    wconv = jnp.concatenate([w_scc, w_sch, w_xbc], axis=-1)

    wq3 = mla_w_q_up[l].reshape(MLA_Q_LORA, MLA_HEADS, MLA_QK_DIM)
    wq_rope = wq3[..., MLA_NOPE_DIM:]
    wq = jnp.concatenate([wq3, _swap_halves(wq_rope)], axis=-1).reshape(MLA_Q_LORA, -1)
    wkv3 = mla_w_kv_up[l].reshape(MLA_KV_LORA, MLA_HEADS, MLA_NOPE_DIM + MLA_V_DIM)
    wk = jnp.concatenate([wkv3[..., :MLA_NOPE_DIM], jnp.zeros_like(wkv3[..., :MLA_NOPE_DIM])],
                         axis=-1).reshape(MLA_KV_LORA, -1)
    wv = wkv3[..., MLA_NOPE_DIM:].reshape(MLA_KV_LORA, -1)

    pad_heads = lambda a: jnp.concatenate(
        [a[l].astype(_F32), jnp.zeros((LANES - SSD_HEADS,), _F32)]).reshape(1, LANES)
    return {
        "gpre": row(norm_mix_pre), "wmain": wmain.astype(_BF16), "wconv": wconv.astype(_BF16),
        "qn": row(mla_q_norm), "wq": wq.astype(_BF16), "kvn": row(mla_kv_norm),
        "wk": wk.astype(_BF16), "wv": wv.astype(_BF16),
        "scw": sc_conv_w[l].astype(_F32), "ssw": ssd_conv_w[l].astype(_F32),
        "ssb": row(ssd_conv_b), "dtb": pad_heads(ssd_dt_bias),
        "aneg": -jnp.exp(pad_heads(ssd_a_log)),
        "dskip": jnp.repeat(ssd_d[l].astype(_F32), SSD_HEAD_DIM).reshape(1, SSD_DIM),
        "snorm": row(ssd_norm),
        "woa": w_out[l, :MLA_OUT].astype(_BF16), "woc": w_out[l, MLA_OUT:].astype(_BF16),
        "gpost": row(norm_mix_post), "gfpre": row(norm_ffn_pre),
        "wg": ffn_w_up[l, :, :FFN_DIM].astype(_BF16), "wu": ffn_w_up[l, :, FFN_DIM:].astype(_BF16),
        "cwg": ffn_conv_w[l, :, :FFN_DIM].astype(_F32), "cwu": ffn_conv_w[l, :, FFN_DIM:].astype(_F32),
        "cbg": ffn_conv_b[l, :FFN_DIM].reshape(1, -1).astype(_F32),
        "cbu": ffn_conv_b[l, FFN_DIM:].reshape(1, -1).astype(_F32),
        "wd": ffn_w_down[l].astype(_BF16), "gfpost": row(norm_ffn_post),
    }


def _rope_tables(positions):
    inv_freq = 1.0 / (ROPE_THETA ** (jnp.arange(0, MLA_ROPE_DIM, 2, dtype=_F32) / MLA_ROPE_DIM))
    ang = positions.astype(_F32)[..., None] * inv_freq
    cos = jnp.cos(ang)
    sin = jnp.sin(ang)
    lead = positions.shape + (MLA_NOPE_DIM,)
    tail = positions.shape + (LANES - MLA_NOPE_DIM - MLA_ROPE_DIM,)
    c1 = jnp.concatenate([jnp.ones(lead, _F32), cos, cos, jnp.zeros(tail, _F32)], axis=-1)
    c2 = jnp.concatenate([jnp.zeros(lead, _F32), -sin, sin, jnp.zeros(tail, _F32)], axis=-1)
    return c1, c2


def kernel(x, positions, norm_mix_pre, norm_mix_post, norm_ffn_pre, norm_ffn_post, w_in, mla_q_norm, mla_w_q_up, mla_kv_norm, mla_w_kv_up, sc_conv_w, ssd_conv_w, ssd_conv_b, ssd_dt_bias, ssd_a_log, ssd_d, ssd_norm, w_out, ffn_w_up, ffn_conv_w, ffn_conv_b, ffn_w_down):
    depth = w_in.shape[0]
    s = x.shape[1]
    ts = min(SEQ_TILE_MIX, s)
    tq = min(SEQ_TILE_ATT_Q, s)
    tk = min(SEQ_TILE_ATT_K, tq)
    tm = min(SEQ_TILE_FFN, s)
    assert s % ts == 0 and s % tq == 0 and tq % tk == 0 and s % tm == 0 and ts % SSD_CHUNK == 0
    c1, c2 = _rope_tables(positions)
    for l in range(depth):
        p = _pack_layer(l, norm_mix_pre, norm_mix_post, norm_ffn_pre, norm_ffn_post, w_in,
                        mla_q_norm, mla_w_q_up, mla_kv_norm, mla_w_kv_up, sc_conv_w, ssd_conv_w,
                        ssd_conv_b, ssd_dt_bias, ssd_a_log, ssd_d, ssd_norm, w_out, ffn_w_up,
                        ffn_conv_w, ffn_conv_b, ffn_w_down)
        qt, k, vt, ycs = _mixer_in(x, c1, c2, p, ts)
        att = _attention(qt, k, vt, tq, tk)
        x = _out_ffn(x, att, ycs, p, tm, FFN_CHUNK)
    return x
```

```python
import functools
import math

import jax
import jax.numpy as jnp
import numpy as np
from jax import lax
from jax.experimental import pallas as pl
from jax.experimental.pallas import tpu as pltpu

D_MODEL = 1024
MLA_HEADS = 8
MLA_Q_LORA = 256
MLA_KV_LORA = 128
MLA_NOPE_DIM = 64
MLA_ROPE_DIM = 32
MLA_V_DIM = 64
ROPE_THETA = 10000.0
SC_DIM = 256
SC_WIDTH = 3
SSD_HEADS = 4
SSD_HEAD_DIM = 64
SSD_GROUPS = 2
SSD_STATE = 128
SSD_CONV_WIDTH = 4
SSD_CHUNK = 128
FFN_DIM = 2816
FFN_CONV_WIDTH = 3
NORM_EPS = 1e-6

MLA_QK_DIM = MLA_NOPE_DIM + MLA_ROPE_DIM
MLA_OUT = MLA_HEADS * MLA_V_DIM
SSD_DIM = SSD_HEADS * SSD_HEAD_DIM
SSD_BC_DIM = SSD_GROUPS * SSD_STATE
SSD_CONV_DIM = SSD_DIM + 2 * SSD_BC_DIM
SSD_IN = SSD_DIM + SSD_CONV_DIM + SSD_HEADS
IN_WIDTHS = (MLA_Q_LORA, MLA_KV_LORA, MLA_ROPE_DIM, SC_DIM, SC_DIM, SC_DIM, SSD_IN)
D_IN = sum(IN_WIDTHS)

LANES = 128
HEAD_PAD = LANES
HALO = 16

_M_CQ = 0
_M_CKV = _M_CQ + MLA_Q_LORA
_M_KR = _M_CKV + MLA_KV_LORA
_M_SCB = _M_KR + LANES
_M_Z = _M_SCB + SC_DIM
_M_DT = _M_Z + SSD_DIM
_M_END = _M_DT + LANES
_C_SCC = 0
_C_SCH = _C_SCC + SC_DIM
_C_XBC = _C_SCH + SC_DIM
_C_END = _C_XBC + SSD_CONV_DIM

SEQ_TILE_MIX = 512
SEQ_TILE_ATT_Q = 512
SEQ_TILE_ATT_K = 512
ATT_SUM_ROWS = 16
SEQ_TILE_FFN = 512
FFN_CHUNK = 1408
VMEM_LIMIT = 56 * 1024 * 1024

_F32 = jnp.float32
_BF16 = jnp.bfloat16
_NT = (((1,), (1,)), ((), ()))
_TN = (((0,), (0,)), ((), ()))


def _rms(x, w):
    return x * lax.rsqrt(jnp.mean(x * x, axis=-1, keepdims=True) + NORM_EPS) * w


def _dot(a, b):
    return jnp.dot(a, b, preferred_element_type=_F32)


def _rope_block(a, c1, c2):
    return a * c1 + pltpu.roll(a, LANES - MLA_ROPE_DIM, 1) * c2


def _mixer_in_kernel(x_ref, c1_ref, c2_ref, gpre_ref, wmain_ref, wconv_ref, qn_ref, wq_ref,
                     kvn_ref, wk_ref, wv_ref, scw_ref, ssw_ref, ssb_ref, dtb_ref, aneg_ref,
                     dskip_ref, snorm_ref,
                     qt_ref, k_ref, vt_ref, ycs_ref,
                     hbuf, pcbuf, ubuf, ybuf, state_ref, *, ts):
    si = pl.program_id(1)

    @pl.when(si == 0)
    def _():
        hbuf[0:HALO, :] = jnp.zeros((HALO, D_MODEL), _BF16)
        state_ref[...] = jnp.zeros_like(state_ref)

    h = _rms(x_ref[...], gpre_ref[...]).astype(_BF16)
    hbuf[HALO:, :] = h
    pm = _dot(h, wmain_ref[...])
    pcbuf[...] = _dot(hbuf[...], wconv_ref[...])
    hbuf[0:HALO, :] = h[ts - HALO:, :]

    c1 = c1_ref[...]
    c2 = c2_ref[...]
    scale = MLA_QK_DIM ** -0.5 * math.log2(math.e)

    qn = _rms(pm[:, _M_CQ:_M_CQ + MLA_Q_LORA], qn_ref[...]).astype(_BF16)
    qfull = _dot(qn, wq_ref[...])
    kvn = _rms(pm[:, _M_CKV:_M_CKV + MLA_KV_LORA], kvn_ref[...]).astype(_BF16)
    kfull = _dot(kvn, wk_ref[...])
    vt_ref[...] = _dot(kvn, wv_ref[...]).T.astype(_BF16)
    kr = _rope_block(pm[:, _M_KR:_M_KR + LANES], c1, c2)
    for hd in range(MLA_HEADS):
        sl = slice(hd * HEAD_PAD, (hd + 1) * HEAD_PAD)
        qt_ref[sl, :] = (_rope_block(qfull[:, sl], c1, c2) * scale).T.astype(_BF16)
        k_ref[:, sl] = (kfull[:, sl] + kr).astype(_BF16)

    ubuf[...] = pcbuf[:, _C_SCC:_C_SCC + SC_DIM] * pcbuf[:, _C_SCH:_C_SCH + SC_DIM]
    conv = ubuf[HALO - SC_WIDTH + 1:HALO - SC_WIDTH + 1 + ts, :] * scw_ref[0:1, :]
    for i in range(1, SC_WIDTH):
        o = HALO - SC_WIDTH + 1 + i
        conv = conv + ubuf[o:o + ts, :] * scw_ref[i:i + 1, :]
    ycs_ref[:, 0:SC_DIM] = (pm[:, _M_SCB:_M_SCB + SC_DIM] * conv).astype(_BF16)

    o0 = HALO - SSD_CONV_WIDTH + 1
    xbc = pcbuf[o0:o0 + ts, _C_XBC:_C_END] * ssw_ref[0:1, :]
    for i in range(1, SSD_CONV_WIDTH):
        xbc = xbc + pcbuf[o0 + i:o0 + i + ts, _C_XBC:_C_END] * ssw_ref[i:i + 1, :]
    xbc = xbc + ssb_ref[...]
    xbc = xbc * jax.nn.sigmoid(xbc)
    z = pm[:, _M_Z:_M_Z + SSD_DIM]
    dt_raw = pm[:, _M_DT:_M_DT + LANES] + dtb_ref[...]
    dt = jnp.maximum(dt_raw, 0.0) + jnp.log1p(jnp.exp(-jnp.abs(dt_raw)))
    adt = dt * aneg_ref[...]

    L = SSD_CHUNK
    rows = lax.broadcasted_iota(jnp.int32, (L, L), 0)
    cols = lax.broadcasted_iota(jnp.int32, (L, L), 1)
    tri = rows >= cols
    tri_f = tri.astype(_F32)
    hpg = SSD_HEADS // SSD_GROUPS
    for c in range(ts // L):
        r = slice(c * L, (c + 1) * L)
        xs_c = xbc[r, 0:SSD_DIM]
        dt_c = dt[r, :]
        a_cs = jnp.dot(tri_f, adt[r, :], preferred_element_type=_F32,
                       precision=lax.Precision.HIGHEST)
        a_cs_t = a_cs.T
        for g in range(SSD_GROUPS):
            bg = xbc[r, SSD_DIM + g * SSD_STATE:SSD_DIM + (g + 1) * SSD_STATE]
            cg = xbc[r, SSD_DIM + SSD_BC_DIM + g * SSD_STATE:
                     SSD_DIM + SSD_BC_DIM + (g + 1) * SSD_STATE].astype(_BF16)
            sc = lax.dot_general(cg, bg.astype(_BF16), _NT, preferred_element_type=_F32)
            for hh in range(g * hpg, (g + 1) * hpg):
                hs = slice(hh * SSD_HEAD_DIM, (hh + 1) * SSD_HEAD_DIM)
                col = a_cs[:, hh:hh + 1]
                row = a_cs_t[hh:hh + 1, :]
                a_end = a_cs[L - 1:L, hh:hh + 1]
                dec = jnp.exp(jnp.where(tri, col - row, -jnp.inf))
                xs_h = xs_c[:, hs]
                xdt = (xs_h * dt_c[:, hh:hh + 1]).astype(_BF16)
                y = _dot((sc * dec).astype(_BF16), xdt)
                st = state_ref[hh]
                y = y + _dot(cg, st.astype(_BF16)) * jnp.exp(col)
                bd = (bg * jnp.exp(a_end - col)).astype(_BF16)
                cs = lax.dot_general(bd, xdt, _TN, preferred_element_type=_F32)
                state_ref[hh] = st * jnp.exp(a_end) + cs
                ybuf[r, hs] = y + xs_h * dskip_ref[:, hs]
    yg = ybuf[...] * (z * jax.nn.sigmoid(z))
    ycs_ref[:, SC_DIM:SC_DIM + SSD_DIM] = _rms(yg, snorm_ref[...]).astype(_BF16)


def _mixer_in(x, c1, c2, p, ts):
    b, s, d = x.shape
    grid = (b, s // ts)
    tile = lambda w: pl.BlockSpec((None, ts, w), lambda bi, si: (bi, si, 0))
    tile_t = lambda w: pl.BlockSpec((None, w, ts), lambda bi, si: (bi, 0, si))
    full = lambda a: pl.BlockSpec(a.shape, lambda bi, si: (0,) * a.ndim)
    weights = [p["gpre"], p["wmain"], p["wconv"], p["qn"], p["wq"], p["kvn"], p["wk"], p["wv"],
               p["scw"], p["ssw"], p["ssb"], p["dtb"], p["aneg"], p["dskip"], p["snorm"]]
    out_shape = [
        jax.ShapeDtypeStruct((b, MLA_HEADS * HEAD_PAD, s), _BF16),
        jax.ShapeDtypeStruct((b, s, MLA_HEADS * HEAD_PAD), _BF16),
        jax.ShapeDtypeStruct((b, MLA_OUT, s), _BF16),
        jax.ShapeDtypeStruct((b, s, SC_DIM + SSD_DIM), _BF16),
    ]
    return pl.pallas_call(
        functools.partial(_mixer_in_kernel, ts=ts),
        grid=grid,
        in_specs=[tile(d), tile(LANES), tile(LANES)] + [full(w) for w in weights],
        out_specs=[tile_t(MLA_HEADS * HEAD_PAD), tile(MLA_HEADS * HEAD_PAD), tile_t(MLA_OUT),
                   tile(SC_DIM + SSD_DIM)],
        out_shape=out_shape,
        scratch_shapes=[
            pltpu.VMEM((ts + HALO, D_MODEL), _BF16),
            pltpu.VMEM((ts + HALO, _C_END), _F32),
            pltpu.VMEM((ts + HALO, SC_DIM), _F32),
            pltpu.VMEM((ts, SSD_DIM), _F32),
            pltpu.VMEM((SSD_HEADS, SSD_STATE, SSD_HEAD_DIM), _F32),
        ],
        compiler_params=pltpu.CompilerParams(
            dimension_semantics=("arbitrary", "arbitrary"), vmem_limit_bytes=VMEM_LIMIT),
        name="mixer_in",
    )(x, c1, c2, *weights)


def _attention_kernel(qt_ref, k_ref, vt_ref, o_ref, s0_ref, s1_ref, m_ref, acc_ref, *, tq, tk):
    qi = pl.program_id(1)
    causal = (lax.broadcasted_iota(jnp.int32, (tk, tq), 0)
              <= lax.broadcasted_iota(jnp.int32, (tk, tq), 1))
    ones = jnp.ones((ATT_SUM_ROWS, tk), _BF16)
    sbufs = (s0_ref, s1_ref)

    def scores(hd, j):
        start = pl.multiple_of(j * tk, tk)
        ks = k_ref[pl.ds(start, tk), hd * HEAD_PAD:(hd + 1) * HEAD_PAD]
        sbufs[hd % 2][...] = _dot(ks, qt_ref[hd * HEAD_PAD:(hd + 1) * HEAD_PAD, :])

    def softmax_pv(hd, j, diagonal):
        st = sbufs[hd % 2][...]
        if diagonal:
            st = jnp.where(causal, st, -jnp.inf)
        m = m_ref[hd]
        m_new = jnp.maximum(m, jnp.max(st, axis=0, keepdims=True))
        alpha = jnp.exp2(m - m_new)
        pt = jnp.exp2(st - m_new).astype(_BF16)
        start = pl.multiple_of(j * tk, tk)
        vt = vt_ref[hd * MLA_V_DIM:(hd + 1) * MLA_V_DIM, pl.ds(start, tk)]
        lhs = jnp.concatenate([vt, ones], axis=0)
        acc = alpha * acc_ref[hd] + _dot(lhs, pt)
        if diagonal:
            o = acc[0:MLA_V_DIM, :] / acc[MLA_V_DIM:MLA_V_DIM + 1, :]
            o_ref[:, hd * MLA_V_DIM:(hd + 1) * MLA_V_DIM] = o.T.astype(_BF16)
        else:
            m_ref[hd] = m_new
            acc_ref[hd] = acc

    m_ref[...] = jnp.full(m_ref.shape, -jnp.inf, _F32)
    acc_ref[...] = jnp.zeros(acc_ref.shape, _F32)
    scores(0, 0)

    def body(j, carry):
        for hd in range(MLA_HEADS):
            if hd + 1 < MLA_HEADS:
                scores(hd + 1, j)
            else:
                scores(0, j + 1)
            softmax_pv(hd, j, False)
        return carry

    lax.fori_loop(0, qi, body, 0)
    for hd in range(MLA_HEADS):
        if hd + 1 < MLA_HEADS:
            scores(hd + 1, qi)
        softmax_pv(hd, qi, True)


def _attention(qt, k, vt, tq, tk):
    b, s, _ = k.shape
    assert tq == tk
    return pl.pallas_call(
        functools.partial(_attention_kernel, tq=tq, tk=tk),
        grid=(b, s // tq),
        in_specs=[
            pl.BlockSpec((None, MLA_HEADS * HEAD_PAD, tq), lambda bi, qi: (bi, 0, qi)),
            pl.BlockSpec((None, s, MLA_HEADS * HEAD_PAD), lambda bi, qi: (bi, 0, 0)),
            pl.BlockSpec((None, MLA_OUT, s), lambda bi, qi: (bi, 0, 0)),
        ],
        out_specs=pl.BlockSpec((None, tq, MLA_OUT), lambda bi, qi: (bi, qi, 0)),
        out_shape=jax.ShapeDtypeStruct((b, s, MLA_OUT), _BF16),
        scratch_shapes=[
            pltpu.VMEM((tk, tq), _F32), pltpu.VMEM((tk, tq), _F32),
            pltpu.VMEM((MLA_HEADS, 1, tq), _F32),
            pltpu.VMEM((MLA_HEADS, MLA_V_DIM + ATT_SUM_ROWS, tq), _F32),
        ],
        compiler_params=pltpu.CompilerParams(
            dimension_semantics=("arbitrary", "arbitrary"), vmem_limit_bytes=VMEM_LIMIT),
        name="attention",
    )(qt, k, vt)


def _out_ffn_kernel(x_ref, att_ref, ycs_ref, woa_ref, woc_ref, gpost_ref, gfpre_ref, wg_ref,
                    wu_ref, cwg_ref, cwu_ref, cbg_ref, cbu_ref, wd_ref, gfpost_ref,
                    o_ref, hbuf, gbuf, ubuf, *, tm, fc):
    si = pl.program_id(1)

    @pl.when(si == 0)
    def _():
        hbuf[0:HALO, :] = jnp.zeros((HALO, D_MODEL), _BF16)

    mixed = _dot(att_ref[...], woa_ref[...]) + _dot(ycs_ref[...], woc_ref[...])
    x1 = x_ref[...] + _rms(mixed, gpost_ref[...])
    h = _rms(x1, gfpre_ref[...]).astype(_BF16)
    hbuf[HALO:, :] = h
    acc = jnp.zeros((tm, D_MODEL), _F32)
    o0 = HALO - FFN_CONV_WIDTH + 1
    for c in range(FFN_DIM // fc):
        cs = slice(c * fc, (c + 1) * fc)
        gbuf[...] = _dot(hbuf[...], wg_ref[:, cs])
        ubuf[...] = _dot(hbuf[...], wu_ref[:, cs])
        g = gbuf[o0:o0 + tm, :] * cwg_ref[0:1, cs]
        u = ubuf[o0:o0 + tm, :] * cwu_ref[0:1, cs]
        for i in range(1, FFN_CONV_WIDTH):
            g = g + gbuf[o0 + i:o0 + i + tm, :] * cwg_ref[i:i + 1, cs]
            u = u + ubuf[o0 + i:o0 + i + tm, :] * cwu_ref[i:i + 1, cs]
        g = g + cbg_ref[:, cs]
        u = u + cbu_ref[:, cs]
        act = (g * jax.nn.sigmoid(g) * u).astype(_BF16)
        acc = acc + _dot(act, wd_ref[cs, :])
    hbuf[0:HALO, :] = h[tm - HALO:, :]
    o_ref[...] = x1 + _rms(acc, gfpost_ref[...])


def _out_ffn(x, att, ycs, p, tm, fc):
    b, s, d = x.shape
    tile = lambda w: pl.BlockSpec((None, tm, w), lambda bi, si: (bi, si, 0))
    full = lambda a: pl.BlockSpec(a.shape, lambda bi, si: (0,) * a.ndim,
                                  pipeline_mode=pl.Buffered(1))
    weights = [p["woa"], p["woc"], p["gpost"], p["gfpre"], p["wg"], p["wu"], p["cwg"], p["cwu"],
               p["cbg"], p["cbu"], p["wd"], p["gfpost"]]
    return pl.pallas_call(
        functools.partial(_out_ffn_kernel, tm=tm, fc=fc),
        grid=(b, s // tm),
        in_specs=[tile(d), tile(MLA_OUT), tile(SC_DIM + SSD_DIM)] + [full(w) for w in weights],
        out_specs=tile(d),
        out_shape=jax.ShapeDtypeStruct((b, s, d), _F32),
        scratch_shapes=[
            pltpu.VMEM((tm + HALO, D_MODEL), _BF16),
            pltpu.VMEM((tm + HALO, fc), _F32),
            pltpu.VMEM((tm + HALO, fc), _F32),
        ],
        compiler_params=pltpu.CompilerParams(
            dimension_semantics=("arbitrary", "arbitrary"), vmem_limit_bytes=VMEM_LIMIT),
        name="out_ffn",
    )(x, att, ycs, *weights)


def _swap_halves(w):
    half = w.shape[-1] // 2
    return jnp.concatenate([w[..., half:], w[..., :half]], axis=-1)


def _pack_layer(l, norm_mix_pre, norm_mix_post, norm_ffn_pre, norm_ffn_post, w_in, mla_q_norm,
                mla_w_q_up, mla_kv_norm, mla_w_kv_up, sc_conv_w, ssd_conv_w, ssd_conv_b,
                ssd_dt_bias, ssd_a_log, ssd_d, ssd_norm, w_out, ffn_w_up, ffn_conv_w, ffn_conv_b,
                ffn_w_down):
    row = lambda a: a[l].reshape(1, -1).astype(_F32)
    splits = np.cumsum(IN_WIDTHS)[:-1]
    w_cq, w_ckv, w_kr, w_scb, w_scc, w_sch, w_ssd = jnp.split(w_in[l], splits, axis=-1)
    w_z = w_ssd[:, :SSD_DIM]
    w_xbc = w_ssd[:, SSD_DIM:SSD_DIM + SSD_CONV_DIM]
    w_dt = w_ssd[:, SSD_DIM + SSD_CONV_DIM:]
    zeros = lambda n: jnp.zeros((D_MODEL, n), w_in.dtype)
    wmain = jnp.concatenate(
        [w_cq, w_ckv, zeros(MLA_NOPE_DIM), w_kr, _swap_halves(w_kr), w_scb, w_z, w_dt,
         zeros(LANES - SSD_HEADS)], axis=-1)
    wconv = jnp.concatenate([w_scc, w_sch, w_xbc], axis=-1)

    wq3 = mla_w_q_up[l].reshape(MLA_Q_LORA, MLA_HEADS, MLA_QK_DIM)
    wq_rope = wq3[..., MLA_NOPE_DIM:]
    wq = jnp.concatenate([wq3, _swap_halves(wq_rope)], axis=-1).reshape(MLA_Q_LORA, -1)
    wkv3 = mla_w_kv_up[l].reshape(MLA_KV_LORA, MLA_HEADS, MLA_NOPE_DIM + MLA_V_DIM)
    wk = jnp.concatenate([wkv3[..., :MLA_NOPE_DIM], jnp.zeros_like(wkv3[..., :MLA_NOPE_DIM])],
                         axis=-1).reshape(MLA_KV_LORA, -1)
    wv = wkv3[..., MLA_NOPE_DIM:].reshape(MLA_KV_LORA, -1)

    pad_heads = lambda a: jnp.concatenate(
        [a[l].astype(_F32), jnp.zeros((LANES - SSD_HEADS,), _F32)]).reshape(1, LANES)
    return {
        "gpre": row(norm_mix_pre), "wmain": wmain.astype(_BF16), "wconv": wconv.astype(_BF16),
        "qn": row(mla_q_norm), "wq": wq.astype(_BF16), "kvn": row(mla_kv_norm),
        "wk": wk.astype(_BF16), "wv": wv.astype(_BF16),
        "scw": sc_conv_w[l].astype(_F32), "ssw": ssd_conv_w[l].astype(_F32),
        "ssb": row(ssd_conv_b), "dtb": pad_heads(ssd_dt_bias),
        "aneg": -jnp.exp(pad_heads(ssd_a_log)),
        "dskip": jnp.repeat(ssd_d[l].astype(_F32), SSD_HEAD_DIM).reshape(1, SSD_DIM),
        "snorm": row(ssd_norm),
        "woa": w_out[l, :MLA_OUT].astype(_BF16), "woc": w_out[l, MLA_OUT:].astype(_BF16),
        "gpost": row(norm_mix_post), "gfpre": row(norm_ffn_pre),
        "wg": ffn_w_up[l, :, :FFN_DIM].astype(_BF16), "wu": ffn_w_up[l, :, FFN_DIM:].astype(_BF16),
        "cwg": ffn_conv_w[l, :, :FFN_DIM].astype(_F32), "cwu": ffn_conv_w[l, :, FFN_DIM:].astype(_F32),
        "cbg": ffn_conv_b[l, :FFN_DIM].reshape(1, -1).astype(_F32),
        "cbu": ffn_conv_b[l, FFN_DIM:].reshape(1, -1).astype(_F32),
        "wd": ffn_w_down[l].astype(_BF16), "gfpost": row(norm_ffn_post),
    }


def _rope_tables(positions):
    inv_freq = 1.0 / (ROPE_THETA ** (jnp.arange(0, MLA_ROPE_DIM, 2, dtype=_F32) / MLA_ROPE_DIM))
    ang = positions.astype(_F32)[..., None] * inv_freq
    cos = jnp.cos(ang)
    sin = jnp.sin(ang)
    lead = positions.shape + (MLA_NOPE_DIM,)
    tail = positions.shape + (LANES - MLA_NOPE_DIM - MLA_ROPE_DIM,)
    c1 = jnp.concatenate([jnp.ones(lead, _F32), cos, cos, jnp.zeros(tail, _F32)], axis=-1)
    c2 = jnp.concatenate([jnp.zeros(lead, _F32), -sin, sin, jnp.zeros(tail, _F32)], axis=-1)
    return c1, c2


def kernel(x, positions, norm_mix_pre, norm_mix_post, norm_ffn_pre, norm_ffn_post, w_in, mla_q_norm, mla_w_q_up, mla_kv_norm, mla_w_kv_up, sc_conv_w, ssd_conv_w, ssd_conv_b, ssd_dt_bias, ssd_a_log, ssd_d, ssd_norm, w_out, ffn_w_up, ffn_conv_w, ffn_conv_b, ffn_w_down):
    depth = w_in.shape[0]
    s = x.shape[1]
    ts = min(SEQ_TILE_MIX, s)
    tq = min(SEQ_TILE_ATT_Q, s)
    tk = min(SEQ_TILE_ATT_K, tq)
    tm = min(SEQ_TILE_FFN, s)
    assert s % ts == 0 and s % tq == 0 and tq % tk == 0 and s % tm == 0 and ts % SSD_CHUNK == 0
    c1, c2 = _rope_tables(positions)
    for l in range(depth):
        p = _pack_layer(l, norm_mix_pre, norm_mix_post, norm_ffn_pre, norm_ffn_post, w_in,
                        mla_q_norm, mla_w_q_up, mla_kv_norm, mla_w_kv_up, sc_conv_w, ssd_conv_w,
                        ssd_conv_b, ssd_dt_bias, ssd_a_log, ssd_d, ssd_norm, w_out, ffn_w_up,
                        ffn_conv_w, ffn_conv_b, ffn_w_down)
        qt, k, vt, ycs = _mixer_in(x, c1, c2, p, ts)
        att = _attention(qt, k, vt, tq, tk)
        x = _out_ffn(x, att, ycs, p, tm, FFN_CHUNK)
    return x
```

```python
import functools
import math

import jax
import jax.numpy as jnp
import numpy as np
from jax import lax
from jax.experimental import pallas as pl
from jax.experimental.pallas import tpu as pltpu

D_MODEL = 1024
MLA_HEADS = 8
MLA_Q_LORA = 256
MLA_KV_LORA = 128
MLA_NOPE_DIM = 64
MLA_ROPE_DIM = 32
MLA_V_DIM = 64
ROPE_THETA = 10000.0
SC_DIM = 256
SC_WIDTH = 3
SSD_HEADS = 4
SSD_HEAD_DIM = 64
SSD_GROUPS = 2
SSD_STATE = 128
SSD_CONV_WIDTH = 4
SSD_CHUNK = 128
FFN_DIM = 2816
FFN_CONV_WIDTH = 3
NORM_EPS = 1e-6

MLA_QK_DIM = MLA_NOPE_DIM + MLA_ROPE_DIM
MLA_OUT = MLA_HEADS * MLA_V_DIM
SSD_DIM = SSD_HEADS * SSD_HEAD_DIM
SSD_BC_DIM = SSD_GROUPS * SSD_STATE
SSD_CONV_DIM = SSD_DIM + 2 * SSD_BC_DIM
SSD_IN = SSD_DIM + SSD_CONV_DIM + SSD_HEADS
IN_WIDTHS = (MLA_Q_LORA, MLA_KV_LORA, MLA_ROPE_DIM, SC_DIM, SC_DIM, SC_DIM, SSD_IN)
D_IN = sum(IN_WIDTHS)

LANES = 128
MXU_COLS = 256
HEAD_PAD = LANES
HALO = 16

_M_CQ = 0
_M_CKV = _M_CQ + MLA_Q_LORA
_M_KRDT = _M_CKV + MLA_KV_LORA
_M_SCB = _M_KRDT + LANES
_M_Z = _M_SCB + SC_DIM
_M_END = _M_Z + SSD_DIM
_C_SCC = 0
_C_SCH = _C_SCC + SC_DIM
_C_XBC = _C_SCH + SC_DIM
_C_END = _C_XBC + SSD_CONV_DIM

SEQ_TILE_MIX = 512
SEQ_TILE_ATT = 512
ATT_SUM_ROWS = 16
SEQ_TILE_FFN = 512
FFN_CHUNKS = (6 * MXU_COLS, 5 * MXU_COLS)
VMEM_LIMIT = 56 * 1024 * 1024

assert _M_END % MXU_COLS == 0 and _C_END % MXU_COLS == 0 and sum(FFN_CHUNKS) == FFN_DIM

_F32 = jnp.float32
_BF16 = jnp.bfloat16


def _rms(x, w):
    return x * lax.rsqrt(jnp.mean(x * x, axis=-1, keepdims=True) + NORM_EPS) * w


def _dot(a, b):
    return jnp.dot(a, b, preferred_element_type=_F32)


def _dot_nt(a, b):
    return lax.dot_general(a, b, (((1,), (1,)), ((), ())), preferred_element_type=_F32)


def _dot_tn(a, b):
    return lax.dot_general(a, b, (((0,), (0,)), ((), ())), preferred_element_type=_F32)


def _dot_f32(a, b):
    return jnp.dot(a, b, preferred_element_type=_F32, precision=lax.Precision.HIGHEST)


def _rope_block(a, c1, c2):
    return a * c1 + pltpu.roll(a, LANES - MLA_ROPE_DIM, 1) * c2


def _mixer_in_kernel(x_ref, c1_ref, c2_ref, gpre_ref, wmain_ref, wconv_ref, qn_ref, wq_ref,
                     kvn_ref, wkv_ref, scw_ref, ssw_ref, ssb_ref, dtb_ref, aneg_ref,
                     dskip_ref, snorm_ref,
                     qt_ref, k_ref, vt_ref, ycs_ref,
                     hbuf, pcbuf, ubuf, ybuf, state_ref, *, ts):
    si = pl.program_id(1)

    @pl.when(si == 0)
    def _():
        hbuf[0:HALO, :] = jnp.zeros((HALO, D_MODEL), _BF16)
        state_ref[...] = jnp.zeros_like(state_ref)

    h = _rms(x_ref[...], gpre_ref[...]).astype(_BF16)
    hbuf[HALO:, :] = h
    pm = _dot(h, wmain_ref[...])
    pcbuf[...] = _dot(hbuf[...], wconv_ref[...])
    hbuf[0:HALO, :] = h[ts - HALO:, :]

    c1 = c1_ref[...]
    c2 = c2_ref[...]
    scale = MLA_QK_DIM ** -0.5 * math.log2(math.e)
    lane = lax.broadcasted_iota(jnp.int32, (1, LANES), 1)
    nope_lanes = lane < MLA_NOPE_DIM

    qn = _rms(pm[:, _M_CQ:_M_CQ + MLA_Q_LORA], qn_ref[...]).astype(_BF16)
    qfull = _dot(qn, wq_ref[...])
    kvn = _rms(pm[:, _M_CKV:_M_CKV + MLA_KV_LORA], kvn_ref[...]).astype(_BF16)
    kvfull = _dot(kvn, wkv_ref[...])
    krdt = pm[:, _M_KRDT:_M_KRDT + LANES]
    kr = _rope_block(krdt, jnp.where(nope_lanes, 0.0, c1), c2)
    for hd in range(MLA_HEADS):
        sl = slice(hd * HEAD_PAD, (hd + 1) * HEAD_PAD)
        qt_ref[sl, :] = (_rope_block(qfull[:, sl], c1, c2) * scale).T.astype(_BF16)
        kv_h = kvfull[:, sl]
        k_ref[:, sl] = (jnp.where(nope_lanes, kv_h, 0.0) + kr).astype(_BF16)
        vt_ref[hd * MLA_V_DIM:(hd + 1) * MLA_V_DIM, :] = kv_h.T[MLA_NOPE_DIM:, :].astype(_BF16)

    ubuf[...] = pcbuf[:, _C_SCC:_C_SCC + SC_DIM] * pcbuf[:, _C_SCH:_C_SCH + SC_DIM]
    conv = ubuf[HALO - SC_WIDTH + 1:HALO - SC_WIDTH + 1 + ts, :] * scw_ref[0:1, :]
    for i in range(1, SC_WIDTH):
        o = HALO - SC_WIDTH + 1 + i
        conv = conv + ubuf[o:o + ts, :] * scw_ref[i:i + 1, :]
    ycs_ref[:, 0:SC_DIM] = (pm[:, _M_SCB:_M_SCB + SC_DIM] * conv).astype(_BF16)

    o0 = HALO - SSD_CONV_WIDTH + 1
    xbc = pcbuf[o0:o0 + ts, _C_XBC:_C_END] * ssw_ref[0:1, :]
    for i in range(1, SSD_CONV_WIDTH):
        xbc = xbc + pcbuf[o0 + i:o0 + i + ts, _C_XBC:_C_END] * ssw_ref[i:i + 1, :]
    xbc = xbc + ssb_ref[...]
    xbc = xbc * jax.nn.sigmoid(xbc)
    z = pm[:, _M_Z:_M_Z + SSD_DIM]
    dt_raw = krdt + dtb_ref[...]
    dt = jnp.maximum(dt_raw, 0.0) + jnp.log1p(jnp.exp(-jnp.abs(dt_raw)))
    adt = dt * aneg_ref[...]

    L = SSD_CHUNK
    nchunk = ts // L
    rows = lax.broadcasted_iota(jnp.int32, (L, L), 0)
    cols = lax.broadcasted_iota(jnp.int32, (L, L), 1)
    tri = rows >= cols
    head_lanes = lane < SSD_HEADS
    packed = jnp.where(head_lanes, adt[0:L, :], 0.0)
    for c in range(1, nchunk):
        packed = packed + pltpu.roll(jnp.where(head_lanes, adt[c * L:(c + 1) * L, :], 0.0),
                                     c * SSD_HEADS, 1)
    a_cs = _dot_f32(tri.astype(_F32), packed)
    a_cs_t = a_cs.T
    hpg = SSD_HEADS // SSD_GROUPS
    for c in range(nchunk):
        r = slice(c * L, (c + 1) * L)
        xs_c = xbc[r, 0:SSD_DIM]
        dt_c = dt[r, :]
        for g in range(SSD_GROUPS):
            bg = xbc[r, SSD_DIM + g * SSD_STATE:SSD_DIM + (g + 1) * SSD_STATE]
            cg = xbc[r, SSD_DIM + SSD_BC_DIM + g * SSD_STATE:
                     SSD_DIM + SSD_BC_DIM + (g + 1) * SSD_STATE]
            sc = _dot_nt(cg.astype(_BF16), bg.astype(_BF16))
            for hh in range(g * hpg, (g + 1) * hpg):
                hs = slice(hh * SSD_HEAD_DIM, (hh + 1) * SSD_HEAD_DIM)
                ln = c * SSD_HEADS + hh
                col = a_cs[:, ln:ln + 1]
                row = a_cs_t[ln:ln + 1, :]
                a_end = a_cs[L - 1:L, ln:ln + 1]
                dec = jnp.exp(jnp.where(tri, col - row, -jnp.inf))
                xs_h = xs_c[:, hs]
                xdt = (xs_h * dt_c[:, hh:hh + 1]).astype(_BF16)
                st = state_ref[hh]
                lhs = jnp.concatenate(
                    [(sc * dec).astype(_BF16), (cg * jnp.exp(col)).astype(_BF16)], axis=1)
                y = _dot(lhs, jnp.concatenate([xdt, st.astype(_BF16)], axis=0))
                bd = (bg * jnp.exp(a_end - col)).astype(_BF16)
                state_ref[hh] = st * jnp.exp(a_end) + _dot_tn(bd, xdt)
                ybuf[r, hs] = y + xs_h * dskip_ref[:, hs]
    yg = ybuf[...] * (z * jax.nn.sigmoid(z))
    ycs_ref[:, SC_DIM:SC_DIM + SSD_DIM] = _rms(yg, snorm_ref[...]).astype(_BF16)


def _mixer_in(x, c1, c2, p, ts):
    b, s, d = x.shape
    grid = (b, s // ts)
    tile = lambda w: pl.BlockSpec((None, ts, w), lambda bi, si: (bi, si, 0))
    tile_t = lambda w: pl.BlockSpec((None, w, ts), lambda bi, si: (bi, 0, si))
    full = lambda a: pl.BlockSpec(a.shape, lambda bi, si: (0,) * a.ndim)
    weights = [p["gpre"], p["wmain"], p["wconv"], p["qn"], p["wq"], p["kvn"], p["wkv"],
               p["scw"], p["ssw"], p["ssb"], p["dtb"], p["aneg"], p["dskip"], p["snorm"]]
    out_shape = [
        jax.ShapeDtypeStruct((b, MLA_HEADS * HEAD_PAD, s), _BF16),
        jax.ShapeDtypeStruct((b, s, MLA_HEADS * HEAD_PAD), _BF16),
        jax.ShapeDtypeStruct((b, MLA_OUT, s), _BF16),
        jax.ShapeDtypeStruct((b, s, SC_DIM + SSD_DIM), _BF16),
    ]
    return pl.pallas_call(
        functools.partial(_mixer_in_kernel, ts=ts),
        grid=grid,
        in_specs=[tile(d), tile(LANES), tile(LANES)] + [full(w) for w in weights],
        out_specs=[tile_t(MLA_HEADS * HEAD_PAD), tile(MLA_HEADS * HEAD_PAD), tile_t(MLA_OUT),
                   tile(SC_DIM + SSD_DIM)],
        out_shape=out_shape,
        scratch_shapes=[
            pltpu.VMEM((ts + HALO, D_MODEL), _BF16),
            pltpu.VMEM((ts + HALO, _C_END), _F32),
            pltpu.VMEM((ts + HALO, SC_DIM), _F32),
            pltpu.VMEM((ts, SSD_DIM), _F32),
            pltpu.VMEM((SSD_HEADS, SSD_STATE, SSD_HEAD_DIM), _F32),
        ],
        compiler_params=pltpu.CompilerParams(
            dimension_semantics=("arbitrary", "arbitrary"), vmem_limit_bytes=VMEM_LIMIT),
        name="mixer_in",
    )(x, c1, c2, *weights)


def _attention_kernel(qt_ref, k_ref, vt_ref, o_ref, s0_ref, s1_ref, m_ref, acc_ref, *, t):
    qi = pl.program_id(1)
    half = t // 2
    tri = (lax.broadcasted_iota(jnp.int32, (half, half), 0)
           <= lax.broadcasted_iota(jnp.int32, (half, half), 1))
    sbufs = (s0_ref, s1_ref)
    hsl = lambda hd: slice(hd * HEAD_PAD, (hd + 1) * HEAD_PAD)
    vsl = lambda hd: slice(hd * MLA_V_DIM, (hd + 1) * MLA_V_DIM)

    def online(st, vt, m, acc):
        m_new = jnp.maximum(m, jnp.max(st, axis=0, keepdims=True))
        alpha = jnp.exp2(m - m_new)
        pt = jnp.exp2(st - m_new).astype(_BF16)
        lhs = jnp.concatenate([vt, jnp.ones((ATT_SUM_ROWS, vt.shape[1]), _BF16)], axis=0)
        return m_new, alpha * acc + _dot(lhs, pt)

    def scores(hd, j):
        start = pl.multiple_of(j * t, t)
        sbufs[hd % 2][...] = _dot(k_ref[pl.ds(start, t), hsl(hd)], qt_ref[hsl(hd), :])

    def softmax_pv(hd, j):
        start = pl.multiple_of(j * t, t)
        m, acc = online(sbufs[hd % 2][...], vt_ref[vsl(hd), pl.ds(start, t)],
                        m_ref[hd], acc_ref[hd])
        m_ref[hd] = m
        acc_ref[hd] = acc

    def scores_diag(hd):
        lo = pl.multiple_of(qi * t, t)
        hi = pl.multiple_of(qi * t + half, half)
        sref = sbufs[hd % 2]
        sref[0:half, :] = _dot(k_ref[pl.ds(lo, half), hsl(hd)], qt_ref[hsl(hd), :])
        sref[half:, half:] = _dot(k_ref[pl.ds(hi, half), hsl(hd)], qt_ref[hsl(hd), half:])

    def softmax_pv_diag(hd):
        lo = pl.multiple_of(qi * t, t)
        hi = pl.multiple_of(qi * t + half, half)
        sref = sbufs[hd % 2]
        st = jnp.concatenate([jnp.where(tri, sref[0:half, 0:half], -jnp.inf),
                              sref[0:half, half:]], axis=1)
        m, acc = online(st, vt_ref[vsl(hd), pl.ds(lo, half)], m_ref[hd], acc_ref[hd])
        st = jnp.where(tri, sref[half:, half:], -jnp.inf)
        _, acc_hi = online(st, vt_ref[vsl(hd), pl.ds(hi, half)], m[:, half:], acc[:, half:])
        acc = jnp.concatenate([acc[:, :half], acc_hi], axis=1)
        o = acc[0:MLA_V_DIM, :] / acc[MLA_V_DIM:MLA_V_DIM + 1, :]
        o_ref[:, vsl(hd)] = o.T.astype(_BF16)

    m_ref[...] = jnp.full(m_ref.shape, -jnp.inf, _F32)
    acc_ref[...] = jnp.zeros(acc_ref.shape, _F32)

    scores(0, 0)

    def body(j, carry):
        for hd in range(MLA_HEADS):
            if hd + 1 < MLA_HEADS:
                scores(hd + 1, j)
            else:
                scores(0, j + 1)
            softmax_pv(hd, j)
        return carry

    lax.fori_loop(0, qi, body, 0)
    for hd in range(MLA_HEADS):
        if hd + 1 < MLA_HEADS:
            scores_diag(hd + 1)
        softmax_pv_diag(hd)


def _attention(qt, k, vt, t):
    b, s, _ = k.shape
    return pl.pallas_call(
        functools.partial(_attention_kernel, t=t),
        grid=(b, s // t),
        in_specs=[
            pl.BlockSpec((None, MLA_HEADS * HEAD_PAD, t), lambda bi, qi: (bi, 0, qi)),
            pl.BlockSpec((None, s, MLA_HEADS * HEAD_PAD), lambda bi, qi: (bi, 0, 0)),
            pl.BlockSpec((None, MLA_OUT, s), lambda bi, qi: (bi, 0, 0)),
        ],
        out_specs=pl.BlockSpec((None, t, MLA_OUT), lambda bi, qi: (bi, qi, 0)),
        out_shape=jax.ShapeDtypeStruct((b, s, MLA_OUT), _BF16),
        scratch_shapes=[
            pltpu.VMEM((t, t), _F32), pltpu.VMEM((t, t), _F32),
            pltpu.VMEM((MLA_HEADS, 1, t), _F32),
            pltpu.VMEM((MLA_HEADS, MLA_V_DIM + ATT_SUM_ROWS, t), _F32),
        ],
        compiler_params=pltpu.CompilerParams(
            dimension_semantics=("arbitrary", "arbitrary"), vmem_limit_bytes=VMEM_LIMIT),
        name="attention",
    )(qt, k, vt)


def _out_ffn_kernel(x_ref, att_ref, ycs_ref, woa_ref, woc_ref, gpost_ref, gfpre_ref, wg_ref,
                    wu_ref, cwg_ref, cwu_ref, cbg_ref, cbu_ref, wd_ref, gfpost_ref,
                    o_ref, hbuf, gbuf, ubuf, *, tm):
    si = pl.program_id(1)

    @pl.when(si == 0)
    def _():
        hbuf[0:HALO, :] = jnp.zeros((HALO, D_MODEL), _BF16)

    mixed = _dot(att_ref[...], woa_ref[...]) + _dot(ycs_ref[...], woc_ref[...])
    x1 = x_ref[...] + _rms(mixed, gpost_ref[...])
    h = _rms(x1, gfpre_ref[...]).astype(_BF16)
    hbuf[HALO:, :] = h
    acc = jnp.zeros((tm, D_MODEL), _F32)
    o0 = HALO - FFN_CONV_WIDTH + 1
    off = 0
    for fc in FFN_CHUNKS:
        cs = slice(off, off + fc)
        off += fc
        gbuf[:, 0:fc] = _dot(hbuf[...], wg_ref[:, cs])
        ubuf[:, 0:fc] = _dot(hbuf[...], wu_ref[:, cs])
        g = gbuf[o0:o0 + tm, 0:fc] * cwg_ref[0:1, cs]
        u = ubuf[o0:o0 + tm, 0:fc] * cwu_ref[0:1, cs]
        for i in range(1, FFN_CONV_WIDTH):
            g = g + gbuf[o0 + i:o0 + i + tm, 0:fc] * cwg_ref[i:i + 1, cs]
            u = u + ubuf[o0 + i:o0 + i + tm, 0:fc] * cwu_ref[i:i + 1, cs]
        g = g + cbg_ref[:, cs]
        u = u + cbu_ref[:, cs]
        act = (g * jax.nn.sigmoid(g) * u).astype(_BF16)
        acc = acc + _dot(act, wd_ref[cs, :])
    hbuf[0:HALO, :] = h[tm - HALO:, :]
    o_ref[...] = x1 + _rms(acc, gfpost_ref[...])


def _out_ffn(x, att, ycs, p, tm):
    b, s, d = x.shape
    tile = lambda w: pl.BlockSpec((None, tm, w), lambda bi, si: (bi, si, 0))
    full = lambda a: pl.BlockSpec(a.shape, lambda bi, si: (0,) * a.ndim,
                                  pipeline_mode=pl.Buffered(1))
    weights = [p["woa"], p["woc"], p["gpost"], p["gfpre"], p["wg"], p["wu"], p["cwg"], p["cwu"],
               p["cbg"], p["cbu"], p["wd"], p["gfpost"]]
    return pl.pallas_call(
        functools.partial(_out_ffn_kernel, tm=tm),
        grid=(b, s // tm),
        in_specs=[tile(d), tile(MLA_OUT), tile(SC_DIM + SSD_DIM)] + [full(w) for w in weights],
        out_specs=tile(d),
        out_shape=jax.ShapeDtypeStruct((b, s, d), _F32),
        scratch_shapes=[
            pltpu.VMEM((tm + HALO, D_MODEL), _BF16),
            pltpu.VMEM((tm + HALO, max(FFN_CHUNKS)), _F32),
            pltpu.VMEM((tm + HALO, max(FFN_CHUNKS)), _F32),
        ],
        compiler_params=pltpu.CompilerParams(
            dimension_semantics=("arbitrary", "arbitrary"), vmem_limit_bytes=VMEM_LIMIT),
        name="out_ffn",
    )(x, att, ycs, *weights)


def _swap_halves(w):
    half = w.shape[-1] // 2
    return jnp.concatenate([w[..., half:], w[..., :half]], axis=-1)


def _pack_layer(l, norm_mix_pre, norm_mix_post, norm_ffn_pre, norm_ffn_post, w_in, mla_q_norm,
                mla_w_q_up, mla_kv_norm, mla_w_kv_up, sc_conv_w, ssd_conv_w, ssd_conv_b,
                ssd_dt_bias, ssd_a_log, ssd_d, ssd_norm, w_out, ffn_w_up, ffn_conv_w, ffn_conv_b,
                ffn_w_down):
    row = lambda a: a[l].reshape(1, -1).astype(_F32)
    splits = np.cumsum(IN_WIDTHS)[:-1]
    w_cq, w_ckv, w_kr, w_scb, w_scc, w_sch, w_ssd = jnp.split(w_in[l], splits, axis=-1)
    w_z = w_ssd[:, :SSD_DIM]
    w_xbc = w_ssd[:, SSD_DIM:SSD_DIM + SSD_CONV_DIM]
    w_dt = w_ssd[:, SSD_DIM + SSD_CONV_DIM:]
    zeros = lambda n: jnp.zeros((D_MODEL, n), w_in.dtype)
    wmain = jnp.concatenate(
        [w_cq, w_ckv, w_dt, zeros(MLA_NOPE_DIM - SSD_HEADS), w_kr, _swap_halves(w_kr), w_scb, w_z],
        axis=-1)
    wconv = jnp.concatenate([w_scc, w_sch, w_xbc], axis=-1)

    wq3 = mla_w_q_up[l].reshape(MLA_Q_LORA, MLA_HEADS, MLA_QK_DIM)
    wq_rope = wq3[..., MLA_NOPE_DIM:]
    wq = jnp.concatenate([wq3, _swap_halves(wq_rope)], axis=-1).reshape(MLA_Q_LORA, -1)

    pad_heads = lambda a: jnp.concatenate(
        [a[l].astype(_F32), jnp.zeros((LANES - SSD_HEADS,), _F32)]).reshape(1, LANES)
    return {
        "gpre": row(norm_mix_pre), "wmain": wmain.astype(_BF16), "wconv": wconv.astype(_BF16),
        "qn": row(mla_q_norm), "wq": wq.astype(_BF16), "kvn": row(mla_kv_norm),
        "wkv": mla_w_kv_up[l].astype(_BF16),
        "scw": sc_conv_w[l].astype(_F32), "ssw": ssd_conv_w[l].astype(_F32),
        "ssb": row(ssd_conv_b), "dtb": pad_heads(ssd_dt_bias),
        "aneg": -jnp.exp(pad_heads(ssd_a_log)),
        "dskip": jnp.repeat(ssd_d[l].astype(_F32), SSD_HEAD_DIM).reshape(1, SSD_DIM),
        "snorm": row(ssd_norm),
        "woa": w_out[l, :MLA_OUT].astype(_BF16), "woc": w_out[l, MLA_OUT:].astype(_BF16),
        "gpost": row(norm_mix_post), "gfpre": row(norm_ffn_pre),
        "wg": ffn_w_up[l, :, :FFN_DIM].astype(_BF16), "wu": ffn_w_up[l, :, FFN_DIM:].astype(_BF16),
        "cwg": ffn_conv_w[l, :, :FFN_DIM].astype(_F32), "cwu": ffn_conv_w[l, :, FFN_DIM:].astype(_F32),
        "cbg": ffn_conv_b[l, :FFN_DIM].reshape(1, -1).astype(_F32),
        "cbu": ffn_conv_b[l, FFN_DIM:].reshape(1, -1).astype(_F32),
        "wd": ffn_w_down[l].astype(_BF16), "gfpost": row(norm_ffn_post),
    }


def _rope_tables(positions):
    inv_freq = 1.0 / (ROPE_THETA ** (jnp.arange(0, MLA_ROPE_DIM, 2, dtype=_F32) / MLA_ROPE_DIM))
    ang = positions.astype(_F32)[..., None] * inv_freq
    cos = jnp.cos(ang)
    sin = jnp.sin(ang)
    lead = positions.shape + (MLA_NOPE_DIM,)
    tail = positions.shape + (LANES - MLA_NOPE_DIM - MLA_ROPE_DIM,)
    c1 = jnp.concatenate([jnp.ones(lead, _F32), cos, cos, jnp.zeros(tail, _F32)], axis=-1)
    c2 = jnp.concatenate([jnp.zeros(lead, _F32), -sin, sin, jnp.zeros(tail, _F32)], axis=-1)
    return c1, c2


def kernel(x, positions, norm_mix_pre, norm_mix_post, norm_ffn_pre, norm_ffn_post, w_in, mla_q_norm, mla_w_q_up, mla_kv_norm, mla_w_kv_up, sc_conv_w, ssd_conv_w, ssd_conv_b, ssd_dt_bias, ssd_a_log, ssd_d, ssd_norm, w_out, ffn_w_up, ffn_conv_w, ffn_conv_b, ffn_w_down):
    depth = w_in.shape[0]
    s = x.shape[1]
    ts = min(SEQ_TILE_MIX, s)
    ta = min(SEQ_TILE_ATT, s)
    tm = min(SEQ_TILE_FFN, s)
    assert s % ts == 0 and s % ta == 0 and s % tm == 0 and ts % SSD_CHUNK == 0
    c1, c2 = _rope_tables(positions)
    for l in range(depth):
        p = _pack_layer(l, norm_mix_pre, norm_mix_post, norm_ffn_pre, norm_ffn_post, w_in,
                        mla_q_norm, mla_w_q_up, mla_kv_norm, mla_w_kv_up, sc_conv_w, ssd_conv_w,
                        ssd_conv_b, ssd_dt_bias, ssd_a_log, ssd_d, ssd_norm, w_out, ffn_w_up,
                        ffn_conv_w, ffn_conv_b, ffn_w_down)
        qt, k, vt, ycs = _mixer_in(x, c1, c2, p, ts)
        att = _attention(qt, k, vt, ta)
        x = _out_ffn(x, att, ycs, p, tm)
    return x
```

```python
import functools
import math

import jax
import jax.numpy as jnp
import numpy as np
from jax import lax
from jax.experimental import pallas as pl
from jax.experimental.pallas import tpu as pltpu

D_MODEL = 1024
MLA_HEADS = 8
MLA_Q_LORA = 256
MLA_KV_LORA = 128
MLA_NOPE_DIM = 64
MLA_ROPE_DIM = 32
MLA_V_DIM = 64
ROPE_THETA = 10000.0
SC_DIM = 256
SC_WIDTH = 3
SSD_HEADS = 4
SSD_HEAD_DIM = 64
SSD_GROUPS = 2
SSD_STATE = 128
SSD_CONV_WIDTH = 4
SSD_CHUNK = 128
FFN_DIM = 2816
FFN_CONV_WIDTH = 3
NORM_EPS = 1e-6

MLA_QK_DIM = MLA_NOPE_DIM + MLA_ROPE_DIM
MLA_OUT = MLA_HEADS * MLA_V_DIM
SSD_DIM = SSD_HEADS * SSD_HEAD_DIM
SSD_BC_DIM = SSD_GROUPS * SSD_STATE
SSD_CONV_DIM = SSD_DIM + 2 * SSD_BC_DIM
SSD_IN = SSD_DIM + SSD_CONV_DIM + SSD_HEADS
IN_WIDTHS = (MLA_Q_LORA, MLA_KV_LORA, MLA_ROPE_DIM, SC_DIM, SC_DIM, SC_DIM, SSD_IN)
D_IN = sum(IN_WIDTHS)

LANES = 128
MXU_COLS = 256
HEAD_PAD = LANES
HALO = 16

_M_CQ = 0
_M_CKV = _M_CQ + MLA_Q_LORA
_M_KRDT = _M_CKV + MLA_KV_LORA
_M_SCB = _M_KRDT + LANES
_M_Z = _M_SCB + SC_DIM
_M_END = _M_Z + SSD_DIM
_C_SCC = 0
_C_SCH = _C_SCC + SC_DIM
_C_XBC = _C_SCH + SC_DIM
_C_END = _C_XBC + SSD_CONV_DIM

SEQ_TILE_MIX = 512
SEQ_TILE_ATT = 512
ATT_SUM_ROWS = 16
SEQ_TILE_FFN = 512
FFN_CHUNKS = (FFN_DIM,)
VMEM_LIMIT = 56 * 1024 * 1024

assert _M_END % MXU_COLS == 0 and _C_END % MXU_COLS == 0
assert sum(FFN_CHUNKS) == FFN_DIM and all(c % MXU_COLS == 0 for c in FFN_CHUNKS)

_F32 = jnp.float32
_BF16 = jnp.bfloat16


def _rms(x, w):
    return x * lax.rsqrt(jnp.mean(x * x, axis=-1, keepdims=True) + NORM_EPS) * w


def _dot(a, b):
    return jnp.dot(a, b, preferred_element_type=_F32)


def _dot_nt(a, b):
    return lax.dot_general(a, b, (((1,), (1,)), ((), ())), preferred_element_type=_F32)


def _dot_tn(a, b):
    return lax.dot_general(a, b, (((0,), (0,)), ((), ())), preferred_element_type=_F32)


def _dot_f32(a, b):
    return jnp.dot(a, b, preferred_element_type=_F32, precision=lax.Precision.HIGHEST)


def _causal_taps(buf, w_ref, cols, wcols, rows):
    width = w_ref.shape[0]
    out = buf[HALO:HALO + rows, cols] * w_ref[width - 1:width, wcols]
    for i in range(width - 1):
        o = HALO - width + 1 + i
        out = out + buf[o:o + rows, cols] * w_ref[i:i + 1, wcols]
    return out


def _rope_block(a, c1, c2):
    return a * c1 + pltpu.roll(a, LANES - MLA_ROPE_DIM, 1) * c2


def _mixer_project(x_ref, gpre_ref, wmain_ref, wconv_ref, hbuf, pm_ref, pc_ref, ts):
    hbuf[HALO:, :] = _rms(x_ref[...], gpre_ref[...]).astype(_BF16)
    pieces = []
    for c0 in range(0, _M_END, MXU_COLS):
        def main_piece(cs=slice(c0, c0 + MXU_COLS)):
            pm_ref[:, cs] = _dot(hbuf[HALO:, :], wmain_ref[:, cs])
        pieces.append(main_piece)
    for c0 in range(0, _C_END, MXU_COLS):
        def conv_piece(cs=slice(c0, c0 + MXU_COLS)):
            pc_ref[:, cs] = _dot(hbuf[...], wconv_ref[:, cs])
        pieces.append(conv_piece)

    def keep_halo():
        hbuf[0:HALO, :] = hbuf[ts:ts + HALO, :]
    pieces.append(keep_halo)
    return pieces


def _mixer_heads(pm, pcbuf, c1_ref, c2_ref, c1t_ref, c2t_ref, qn_ref, wqt_ref, kvn_ref, wk_ref,
                 wvt_ref, scw_ref, ssw_ref, ssb_ref, dtb_ref, aneg_ref, dskip_ref, snorm_ref,
                 qt_ref, k_ref, vt_ref, ycs_ref, ubuf, ybuf, state_ref, ts, pieces):
    def tick():
        if pieces:
            pieces.pop(0)()

    scale = MLA_QK_DIM ** -0.5 * math.log2(math.e)
    lane = lax.broadcasted_iota(jnp.int32, (1, LANES), 1)

    tick()
    qn = _rms(pm[:, _M_CQ:_M_CQ + MLA_Q_LORA], qn_ref[...]).astype(_BF16)
    qt = _dot_nt(wqt_ref[...], qn)
    c1t = c1t_ref[...]
    c2t = c2t_ref[...]
    for hd in range(MLA_HEADS):
        a = qt[hd * HEAD_PAD:(hd + 1) * HEAD_PAD, :]
        a_swapped = jnp.concatenate([a[MLA_ROPE_DIM:, :], a[:MLA_ROPE_DIM, :]], axis=0)
        qt_ref[hd * HEAD_PAD:(hd + 1) * HEAD_PAD, :] = (
            (a * c1t + a_swapped * c2t) * scale).astype(_BF16)
        if hd % 4 == 3:
            tick()
    kvn = _rms(pm[:, _M_CKV:_M_CKV + MLA_KV_LORA], kvn_ref[...]).astype(_BF16)
    vt_ref[...] = _dot_nt(wvt_ref[...], kvn).astype(_BF16)
    kfull = _dot(kvn, wk_ref[...])
    krdt = pm[:, _M_KRDT:_M_KRDT + LANES]
    kr = _rope_block(krdt, jnp.where(lane < MLA_NOPE_DIM, 0.0, c1_ref[...]), c2_ref[...])
    for hd in range(MLA_HEADS):
        sl = slice(hd * HEAD_PAD, (hd + 1) * HEAD_PAD)
        k_ref[:, sl] = (kfull[:, sl] + kr).astype(_BF16)
    tick()

    ubuf[...] = pcbuf[:, _C_SCC:_C_SCC + SC_DIM] * pcbuf[:, _C_SCH:_C_SCH + SC_DIM]
    conv = _causal_taps(ubuf, scw_ref, slice(0, SC_DIM), slice(0, SC_DIM), ts)
    ycs_ref[:, 0:SC_DIM] = (pm[:, _M_SCB:_M_SCB + SC_DIM] * conv).astype(_BF16)
    tick()

    xbc = _causal_taps(pcbuf, ssw_ref, slice(_C_XBC, _C_END), slice(0, SSD_CONV_DIM), ts)
    xbc = xbc + ssb_ref[...]
    xbc = xbc * jax.nn.sigmoid(xbc)
    z = pm[:, _M_Z:_M_Z + SSD_DIM]
    dt_raw = krdt + dtb_ref[...]
    dt = jnp.maximum(dt_raw, 0.0) + jnp.log1p(jnp.exp(-jnp.abs(dt_raw)))
    adt = dt * aneg_ref[...]
    tick()

    L = SSD_CHUNK
    nchunk = ts // L
    rows = lax.broadcasted_iota(jnp.int32, (L, L), 0)
    cols = lax.broadcasted_iota(jnp.int32, (L, L), 1)
    tri = rows >= cols
    head_lanes = lane < SSD_HEADS
    packed = jnp.where(head_lanes, adt[0:L, :], 0.0)
    for c in range(1, nchunk):
        packed = packed + pltpu.roll(jnp.where(head_lanes, adt[c * L:(c + 1) * L, :], 0.0),
                                     c * SSD_HEADS, 1)
    a_cs = _dot_f32(tri.astype(_F32), packed)
    a_cs_t = a_cs.T
    hpg = SSD_HEADS // SSD_GROUPS
    states = [state_ref[hh] for hh in range(SSD_HEADS)]
    for c in range(nchunk):
        r = slice(c * L, (c + 1) * L)
        xs_c = xbc[r, 0:SSD_DIM]
        dt_c = dt[r, :]
        for g in range(SSD_GROUPS):
            bg = xbc[r, SSD_DIM + g * SSD_STATE:SSD_DIM + (g + 1) * SSD_STATE]
            cg = xbc[r, SSD_DIM + SSD_BC_DIM + g * SSD_STATE:
                     SSD_DIM + SSD_BC_DIM + (g + 1) * SSD_STATE]
            sc = _dot_nt(cg.astype(_BF16), bg.astype(_BF16))
            for hh in range(g * hpg, (g + 1) * hpg):
                hs = slice(hh * SSD_HEAD_DIM, (hh + 1) * SSD_HEAD_DIM)
                ln = c * SSD_HEADS + hh
                col = a_cs[:, ln:ln + 1]
                row = a_cs_t[ln:ln + 1, :]
                a_end = a_cs[L - 1:L, ln:ln + 1]
                dec = jnp.exp(jnp.where(tri, col - row, -jnp.inf))
                xs_h = xs_c[:, hs]
                xdt = (xs_h * dt_c[:, hh:hh + 1]).astype(_BF16)
                y = _dot((sc * dec).astype(_BF16), xdt)
                y = y + _dot((cg * jnp.exp(col)).astype(_BF16), states[hh].astype(_BF16))
                bd = (bg * jnp.exp(a_end - col)).astype(_BF16)
                states[hh] = states[hh] * jnp.exp(a_end) + _dot_tn(bd, xdt)
                ybuf[r, hs] = y + xs_h * dskip_ref[:, hs]
        tick()
    for hh in range(SSD_HEADS):
        state_ref[hh] = states[hh]
    while pieces:
        tick()
    yg = ybuf[...] * (z * jax.nn.sigmoid(z))
    ycs_ref[:, SC_DIM:SC_DIM + SSD_DIM] = _rms(yg, snorm_ref[...]).astype(_BF16)


def _mixer_in_kernel(x_ref, c1_ref, c2_ref, c1t_ref, c2t_ref, gpre_ref, wmain_ref, wconv_ref,
                     qn_ref, wqt_ref, kvn_ref, wk_ref, wvt_ref, scw_ref, ssw_ref, ssb_ref, dtb_ref,
                     aneg_ref, dskip_ref, snorm_ref,
                     qt_ref, k_ref, vt_ref, ycs_ref,
                     hbuf, pm0, pm1, pc0, pc1, ubuf, ybuf, state_ref, *, ts, nt):
    i = pl.program_id(0)

    @pl.when(i == 0)
    def _():
        pm1[...] = jnp.zeros_like(pm1)
        pc1[...] = jnp.zeros_like(pc1)
        state_ref[...] = jnp.zeros_like(state_ref)

    @pl.when(lax.rem(i, nt) == 0)
    def _():
        hbuf[0:HALO, :] = jnp.zeros((HALO, D_MODEL), _BF16)

    @pl.when(lax.rem(i + nt - 1, nt) == 0)
    def _():
        state_ref[...] = jnp.zeros_like(state_ref)

    def step(pm_w, pc_w, pm_r, pc_r):
        pieces = _mixer_project(x_ref, gpre_ref, wmain_ref, wconv_ref, hbuf, pm_w, pc_w, ts)
        _mixer_heads(pm_r, pc_r, c1_ref, c2_ref, c1t_ref, c2t_ref, qn_ref, wqt_ref, kvn_ref,
                     wk_ref, wvt_ref, scw_ref, ssw_ref, ssb_ref, dtb_ref, aneg_ref, dskip_ref,
                     snorm_ref, qt_ref, k_ref, vt_ref, ycs_ref, ubuf, ybuf, state_ref, ts, pieces)

    @pl.when(lax.rem(i, 2) == 0)
    def _():
        step(pm0, pc0, pm1, pc1)

    @pl.when(lax.rem(i, 2) == 1)
    def _():
        step(pm1, pc1, pm0, pc0)


def _mixer_in(x, rope, p, l, ts):
    b, s, d = x.shape
    nt = s // ts
    ntiles = b * nt
    t_in = lambda i: jnp.minimum(i, ntiles - 1)
    t_out = lambda i: jnp.maximum(i - 1, 0)
    tile = lambda w, t: pl.BlockSpec((None, ts, w), lambda i: (t(i) // nt, t(i) % nt, 0))
    tile_t = lambda w, t: pl.BlockSpec((None, w, ts), lambda i: (t(i) // nt, 0, t(i) % nt))
    layer = lambda a: pl.BlockSpec((None,) + a.shape[1:], lambda i: (l, 0, 0))
    c1, c2, c1t, c2t = rope
    weights = [p["gpre"], p["wmain"], p["wconv"], p["qn"], p["wqt"], p["kvn"], p["wk"], p["wvt"],
               p["scw"], p["ssw"], p["ssb"], p["dtb"], p["aneg"], p["dskip"], p["snorm"]]
    out_shape = [
        jax.ShapeDtypeStruct((b, MLA_HEADS * HEAD_PAD, s), _BF16),
        jax.ShapeDtypeStruct((b, s, MLA_HEADS * HEAD_PAD), _BF16),
        jax.ShapeDtypeStruct((b, MLA_OUT, s), _BF16),
        jax.ShapeDtypeStruct((b, s, SC_DIM + SSD_DIM), _BF16),
    ]
    return pl.pallas_call(
        functools.partial(_mixer_in_kernel, ts=ts, nt=nt),
        grid=(ntiles + 1,),
        in_specs=[tile(d, t_in), tile(LANES, t_out), tile(LANES, t_out),
                  tile_t(LANES, t_out), tile_t(LANES, t_out)]
        + [layer(w) for w in weights],
        out_specs=[tile_t(MLA_HEADS * HEAD_PAD, t_out), tile(MLA_HEADS * HEAD_PAD, t_out),
                   tile_t(MLA_OUT, t_out), tile(SC_DIM + SSD_DIM, t_out)],
        out_shape=out_shape,
        scratch_shapes=[
            pltpu.VMEM((ts + HALO, D_MODEL), _BF16),
            pltpu.VMEM((ts, _M_END), _F32), pltpu.VMEM((ts, _M_END), _F32),
            pltpu.VMEM((ts + HALO, _C_END), _F32), pltpu.VMEM((ts + HALO, _C_END), _F32),
            pltpu.VMEM((ts + HALO, SC_DIM), _F32),
            pltpu.VMEM((ts, SSD_DIM), _F32),
            pltpu.VMEM((SSD_HEADS, SSD_STATE, SSD_HEAD_DIM), _F32),
        ],
        compiler_params=pltpu.CompilerParams(
            dimension_semantics=("arbitrary",), vmem_limit_bytes=VMEM_LIMIT),
        name="mixer_in",
    )(x, c1, c2, c1t, c2t, *weights)


def _attention_kernel(qt_ref, k_ref, vt_ref, o_ref, s0_ref, s1_ref, m_ref, acc_ref, *, t):
    qi = pl.program_id(1)
    half = t // 2
    tri = (lax.broadcasted_iota(jnp.int32, (half, half), 0)
           <= lax.broadcasted_iota(jnp.int32, (half, half), 1))
    sbufs = (s0_ref, s1_ref)
    hsl = lambda hd: slice(hd * HEAD_PAD, (hd + 1) * HEAD_PAD)
    vsl = lambda hd: slice(hd * MLA_V_DIM, (hd + 1) * MLA_V_DIM)

    def online(st, vt, m, acc):
        m_new = jnp.maximum(m, jnp.max(st, axis=0, keepdims=True))
        alpha = jnp.exp2(m - m_new)
        pt = jnp.exp2(st - m_new).astype(_BF16)
        lhs = jnp.concatenate([vt, jnp.ones((ATT_SUM_ROWS, vt.shape[1]), _BF16)], axis=0)
        return m_new, alpha * acc + _dot(lhs, pt)

    def scores(hd, j):
        start = pl.multiple_of(j * t, t)
        sbufs[hd % 2][...] = _dot(k_ref[pl.ds(start, t), hsl(hd)], qt_ref[hsl(hd), :])

    def softmax_pv(hd, j):
        start = pl.multiple_of(j * t, t)
        m, acc = online(sbufs[hd % 2][...], vt_ref[vsl(hd), pl.ds(start, t)],
                        m_ref[hd], acc_ref[hd])
        m_ref[hd] = m
        acc_ref[hd] = acc

    def scores_diag(hd):
        lo = pl.multiple_of(qi * t, t)
        hi = pl.multiple_of(qi * t + half, half)
        sref = sbufs[hd % 2]
        sref[0:half, :] = _dot(k_ref[pl.ds(lo, half), hsl(hd)], qt_ref[hsl(hd), :])
        sref[half:, half:] = _dot(k_ref[pl.ds(hi, half), hsl(hd)], qt_ref[hsl(hd), half:])

    def softmax_pv_diag(hd):
        lo = pl.multiple_of(qi * t, t)
        hi = pl.multiple_of(qi * t + half, half)
        sref = sbufs[hd % 2]
        st = jnp.concatenate([jnp.where(tri, sref[0:half, 0:half], -jnp.inf),
                              sref[0:half, half:]], axis=1)
        m, acc = online(st, vt_ref[vsl(hd), pl.ds(lo, half)], m_ref[hd], acc_ref[hd])
        st = jnp.where(tri, sref[half:, half:], -jnp.inf)
        _, acc_hi = online(st, vt_ref[vsl(hd), pl.ds(hi, half)], m[:, half:], acc[:, half:])
        acc = jnp.concatenate([acc[:, :half], acc_hi], axis=1)
        o = acc[0:MLA_V_DIM, :] / acc[MLA_V_DIM:MLA_V_DIM + 1, :]
        o_ref[:, vsl(hd)] = o.T.astype(_BF16)

    m_ref[...] = jnp.full(m_ref.shape, -jnp.inf, _F32)
    acc_ref[...] = jnp.zeros(acc_ref.shape, _F32)

    scores(0, 0)

    def body(j, carry):
        for hd in range(MLA_HEADS):
            if hd + 1 < MLA_HEADS:
                scores(hd + 1, j)
            else:
                scores(0, j + 1)
            softmax_pv(hd, j)
        return carry

    lax.fori_loop(0, qi, body, 0)
    for hd in range(MLA_HEADS):
        if hd + 1 < MLA_HEADS:
            scores_diag(hd + 1)
        softmax_pv_diag(hd)


def _attention(qt, k, vt, t):
    b, s, _ = k.shape
    return pl.pallas_call(
        functools.partial(_attention_kernel, t=t),
        grid=(b, s // t),
        in_specs=[
            pl.BlockSpec((None, MLA_HEADS * HEAD_PAD, t), lambda bi, qi: (bi, 0, qi)),
            pl.BlockSpec((None, s, MLA_HEADS * HEAD_PAD), lambda bi, qi: (bi, 0, 0)),
            pl.BlockSpec((None, MLA_OUT, s), lambda bi, qi: (bi, 0, 0)),
        ],
        out_specs=pl.BlockSpec((None, t, MLA_OUT), lambda bi, qi: (bi, qi, 0)),
        out_shape=jax.ShapeDtypeStruct((b, s, MLA_OUT), _BF16),
        scratch_shapes=[
            pltpu.VMEM((t, t), _F32), pltpu.VMEM((t, t), _F32),
            pltpu.VMEM((MLA_HEADS, 1, t), _F32),
            pltpu.VMEM((MLA_HEADS, MLA_V_DIM + ATT_SUM_ROWS, t), _F32),
        ],
        compiler_params=pltpu.CompilerParams(
            dimension_semantics=("arbitrary", "arbitrary"), vmem_limit_bytes=VMEM_LIMIT),
        name="attention",
    )(qt, k, vt)


def _out_ffn_kernel(x_ref, att_ref, ycs_ref, woa_ref, woc_ref, gpost_ref, gfpre_ref, wg_ref,
                    wu_ref, cwg_ref, cwu_ref, cbg_ref, cbu_ref, wd_ref, gfpost_ref,
                    o_ref, hbuf, gbuf, ubuf, *, tm):
    si = pl.program_id(1)

    @pl.when(si == 0)
    def _():
        hbuf[0:HALO, :] = jnp.zeros((HALO, D_MODEL), _BF16)

    mixed = _dot(att_ref[...], woa_ref[...]) + _dot(ycs_ref[...], woc_ref[...])
    x1 = x_ref[...] + _rms(mixed, gpost_ref[...])
    h = _rms(x1, gfpre_ref[...]).astype(_BF16)
    hbuf[HALO:, :] = h
    acc = jnp.zeros((tm, D_MODEL), _F32)
    off = 0
    for fc in FFN_CHUNKS:
        cs = slice(off, off + fc)
        off += fc
        gbuf[:, 0:fc] = _dot(hbuf[...], wg_ref[:, cs])
        ubuf[:, 0:fc] = _dot(hbuf[...], wu_ref[:, cs])
        g = _causal_taps(gbuf, cwg_ref, slice(0, fc), cs, tm) + cbg_ref[:, cs]
        u = _causal_taps(ubuf, cwu_ref, slice(0, fc), cs, tm) + cbu_ref[:, cs]
        act = (g * jax.nn.sigmoid(g) * u).astype(_BF16)
        acc = acc + _dot(act, wd_ref[cs, :])
    hbuf[0:HALO, :] = h[tm - HALO:, :]
    o_ref[...] = x1 + _rms(acc, gfpost_ref[...])


def _out_ffn(x, att, ycs, p, l, tm):
    b, s, d = x.shape
    tile = lambda w: pl.BlockSpec((None, tm, w), lambda bi, si: (bi, si, 0))

    def layer(a, axis=None, part=0):
        shape, idx = list(a.shape[1:]), [0, 0]
        if axis is not None:
            shape[axis - 1] //= 2
            idx[axis - 1] = part
        return pl.BlockSpec((None, *shape), lambda bi, si: (l, *idx), pipeline_mode=pl.Buffered(1))

    operands = [
        (p["wo"], layer(p["wo"], 1, 0)), (p["wo"], layer(p["wo"], 1, 1)),
        (p["gpost"], layer(p["gpost"])), (p["gfpre"], layer(p["gfpre"])),
        (p["wup"], layer(p["wup"], 2, 0)), (p["wup"], layer(p["wup"], 2, 1)),
        (p["cw"], layer(p["cw"], 2, 0)), (p["cw"], layer(p["cw"], 2, 1)),
        (p["cb"], layer(p["cb"], 2, 0)), (p["cb"], layer(p["cb"], 2, 1)),
        (p["wd"], layer(p["wd"])), (p["gfpost"], layer(p["gfpost"])),
    ]
    weights = [a for a, _ in operands]
    return pl.pallas_call(
        functools.partial(_out_ffn_kernel, tm=tm),
        grid=(b, s // tm),
        in_specs=[tile(d), tile(MLA_OUT), tile(SC_DIM + SSD_DIM)] + [sp for _, sp in operands],
        out_specs=tile(d),
        out_shape=jax.ShapeDtypeStruct((b, s, d), _F32),
        scratch_shapes=[
            pltpu.VMEM((tm + HALO, D_MODEL), _BF16),
            pltpu.VMEM((tm + HALO, max(FFN_CHUNKS)), _F32),
            pltpu.VMEM((tm + HALO, max(FFN_CHUNKS)), _F32),
        ],
        compiler_params=pltpu.CompilerParams(
            dimension_semantics=("arbitrary", "arbitrary"), vmem_limit_bytes=VMEM_LIMIT),
        name="out_ffn",
    )(x, att, ycs, *weights)


def _swap_halves(w):
    half = w.shape[-1] // 2
    return jnp.concatenate([w[..., half:], w[..., :half]], axis=-1)


def _pack_params(norm_mix_pre, norm_mix_post, norm_ffn_pre, norm_ffn_post, w_in, mla_q_norm,
                 mla_w_q_up, mla_kv_norm, mla_w_kv_up, sc_conv_w, ssd_conv_w, ssd_conv_b,
                 ssd_dt_bias, ssd_a_log, ssd_d, ssd_norm, w_out, ffn_w_up, ffn_conv_w, ffn_conv_b,
                 ffn_w_down):
    depth = w_in.shape[0]
    f32 = lambda a: a.astype(_F32)
    row = lambda a: f32(a)[:, None, :]
    splits = np.cumsum(IN_WIDTHS)[:-1]
    w_cq, w_ckv, w_kr, w_scb, w_scc, w_sch, w_ssd = jnp.split(w_in, splits, axis=-1)
    w_z = w_ssd[..., :SSD_DIM]
    w_xbc = w_ssd[..., SSD_DIM:SSD_DIM + SSD_CONV_DIM]
    w_dt = w_ssd[..., SSD_DIM + SSD_CONV_DIM:]
    zeros = jnp.zeros((depth, D_MODEL, MLA_NOPE_DIM - SSD_HEADS), w_in.dtype)
    wmain = jnp.concatenate(
        [w_cq, w_ckv, w_dt, zeros, w_kr, _swap_halves(w_kr), w_scb, w_z], axis=-1)
    wconv = jnp.concatenate([w_scc, w_sch, w_xbc], axis=-1)

    wq4 = mla_w_q_up.reshape(depth, MLA_Q_LORA, MLA_HEADS, MLA_QK_DIM)
    wq = jnp.concatenate([wq4, _swap_halves(wq4[..., MLA_NOPE_DIM:])], axis=-1)
    wq = wq.reshape(depth, MLA_Q_LORA, MLA_HEADS * HEAD_PAD)
    wkv4 = mla_w_kv_up.reshape(depth, MLA_KV_LORA, MLA_HEADS, MLA_NOPE_DIM + MLA_V_DIM)
    wk = jnp.pad(wkv4[..., :MLA_NOPE_DIM], ((0, 0),) * 3 + ((0, HEAD_PAD - MLA_NOPE_DIM),))
    wk = wk.reshape(depth, MLA_KV_LORA, MLA_HEADS * HEAD_PAD)
    wvt = jnp.swapaxes(wkv4[..., MLA_NOPE_DIM:].reshape(depth, MLA_KV_LORA, MLA_OUT), 1, 2)

    pad_heads = lambda a: jnp.pad(f32(a), ((0, 0), (0, LANES - SSD_HEADS)))[:, None, :]
    return {
        "gpre": row(norm_mix_pre), "wmain": wmain.astype(_BF16), "wconv": wconv.astype(_BF16),
        "qn": row(mla_q_norm), "wqt": jnp.swapaxes(wq, 1, 2).astype(_BF16),
        "kvn": row(mla_kv_norm), "wk": wk.astype(_BF16), "wvt": wvt.astype(_BF16),
        "scw": f32(sc_conv_w), "ssw": f32(ssd_conv_w),
        "ssb": row(ssd_conv_b), "dtb": pad_heads(ssd_dt_bias),
        "aneg": -jnp.exp(pad_heads(ssd_a_log)),
        "dskip": jnp.repeat(f32(ssd_d), SSD_HEAD_DIM, axis=-1)[:, None, :],
        "snorm": row(ssd_norm),
        "wo": w_out.astype(_BF16), "gpost": row(norm_mix_post), "gfpre": row(norm_ffn_pre),
        "wup": ffn_w_up.astype(_BF16), "cw": f32(ffn_conv_w), "cb": row(ffn_conv_b),
        "wd": ffn_w_down.astype(_BF16), "gfpost": row(norm_ffn_post),
    }


def _rope_tables(positions):
    inv_freq = 1.0 / (ROPE_THETA ** (jnp.arange(0, MLA_ROPE_DIM, 2, dtype=_F32) / MLA_ROPE_DIM))
    ang = positions.astype(_F32)[..., None] * inv_freq
    cos = jnp.cos(ang)
    sin = jnp.sin(ang)
    lead = positions.shape + (MLA_NOPE_DIM,)
    tail = positions.shape + (LANES - MLA_NOPE_DIM - MLA_ROPE_DIM,)
    c1 = jnp.concatenate([jnp.ones(lead, _F32), cos, cos, jnp.zeros(tail, _F32)], axis=-1)
    c2 = jnp.concatenate([jnp.zeros(lead, _F32), -sin, sin, jnp.zeros(tail, _F32)], axis=-1)
    return c1, c2, jnp.swapaxes(c1, 1, 2), jnp.swapaxes(c2, 1, 2)


def kernel(x, positions, norm_mix_pre, norm_mix_post, norm_ffn_pre, norm_ffn_post, w_in, mla_q_norm, mla_w_q_up, mla_kv_norm, mla_w_kv_up, sc_conv_w, ssd_conv_w, ssd_conv_b, ssd_dt_bias, ssd_a_log, ssd_d, ssd_norm, w_out, ffn_w_up, ffn_conv_w, ffn_conv_b, ffn_w_down):
    depth = w_in.shape[0]
    s = x.shape[1]
    ts = min(SEQ_TILE_MIX, s)
    ta = min(SEQ_TILE_ATT, s)
    tm = min(SEQ_TILE_FFN, s)
    assert s % ts == 0 and s % ta == 0 and s % tm == 0 and ts % SSD_CHUNK == 0
    rope = _rope_tables(positions)
    p = _pack_params(norm_mix_pre, norm_mix_post, norm_ffn_pre, norm_ffn_post, w_in, mla_q_norm,
                     mla_w_q_up, mla_kv_norm, mla_w_kv_up, sc_conv_w, ssd_conv_w, ssd_conv_b,
                     ssd_dt_bias, ssd_a_log, ssd_d, ssd_norm, w_out, ffn_w_up, ffn_conv_w,
                     ffn_conv_b, ffn_w_down)
    for l in range(depth):
        qt, k, vt, ycs = _mixer_in(x, rope, p, l, ts)
        att = _attention(qt, k, vt, ta)
        x = _out_ffn(x, att, ycs, p, l, tm)
    return x
```

```python
import functools
import math

import jax
import jax.numpy as jnp
import numpy as np
from jax import lax
from jax.experimental import pallas as pl
from jax.experimental.pallas import tpu as pltpu

D_MODEL = 1024
MLA_HEADS = 8
MLA_Q_LORA = 256
MLA_KV_LORA = 128
MLA_NOPE_DIM = 64
MLA_ROPE_DIM = 32
MLA_V_DIM = 64
ROPE_THETA = 10000.0
SC_DIM = 256
SC_WIDTH = 3
SSD_HEADS = 4
SSD_HEAD_DIM = 64
SSD_GROUPS = 2
SSD_STATE = 128
SSD_CONV_WIDTH = 4
SSD_CHUNK = 128
FFN_DIM = 2816
FFN_CONV_WIDTH = 3
NORM_EPS = 1e-6

MLA_QK_DIM = MLA_NOPE_DIM + MLA_ROPE_DIM
MLA_OUT = MLA_HEADS * MLA_V_DIM
SSD_DIM = SSD_HEADS * SSD_HEAD_DIM
SSD_BC_DIM = SSD_GROUPS * SSD_STATE
SSD_CONV_DIM = SSD_DIM + 2 * SSD_BC_DIM
SSD_IN = SSD_DIM + SSD_CONV_DIM + SSD_HEADS
IN_WIDTHS = (MLA_Q_LORA, MLA_KV_LORA, MLA_ROPE_DIM, SC_DIM, SC_DIM, SC_DIM, SSD_IN)
D_IN = sum(IN_WIDTHS)

LANES = 128
MXU_COLS = 256
HEAD_PAD = LANES
HALO = 16

_M_CQ = 0
_M_CKV = _M_CQ + MLA_Q_LORA
_M_KRDT = _M_CKV + MLA_KV_LORA
_M_SCB = _M_KRDT + LANES
_M_Z = _M_SCB + SC_DIM
_M_END = _M_Z + SSD_DIM
_C_SCC = 0
_C_SCH = _C_SCC + SC_DIM
_C_XBC = _C_SCH + SC_DIM
_C_END = _C_XBC + SSD_CONV_DIM

SEQ_TILE_MIX = 512
SEQ_TILE_ATT = 512
ATT_SUM_ROWS = 16
SEQ_TILE_FFN = 512
FFN_CHUNKS = (FFN_DIM,)
VMEM_LIMIT = 56 * 1024 * 1024

assert _M_END % MXU_COLS == 0 and _C_END % MXU_COLS == 0
assert sum(FFN_CHUNKS) == FFN_DIM and all(c % MXU_COLS == 0 for c in FFN_CHUNKS)

_F32 = jnp.float32
_BF16 = jnp.bfloat16


def _rms(x, w):
    return x * lax.rsqrt(jnp.mean(x * x, axis=-1, keepdims=True) + NORM_EPS) * w


def _dot(a, b):
    return jnp.dot(a, b, preferred_element_type=_F32)


def _dot_nt(a, b):
    return lax.dot_general(a, b, (((1,), (1,)), ((), ())), preferred_element_type=_F32)


def _dot_tn(a, b):
    return lax.dot_general(a, b, (((0,), (0,)), ((), ())), preferred_element_type=_F32)


def _dot_f32(a, b):
    return jnp.dot(a, b, preferred_element_type=_F32, precision=lax.Precision.HIGHEST)


def _causal_taps(buf, w_ref, cols, wcols, rows):
    width = w_ref.shape[0]
    out = buf[HALO:HALO + rows, cols] * w_ref[width - 1:width, wcols]
    for i in range(width - 1):
        o = HALO - width + 1 + i
        out = out + buf[o:o + rows, cols] * w_ref[i:i + 1, wcols]
    return out


def _rope_block(a, c1, c2):
    return a * c1 + pltpu.roll(a, LANES - MLA_ROPE_DIM, 1) * c2


def _mixer_project(x_ref, gpre_ref, wmain_ref, wconv_ref, hbuf, pm_ref, pc_ref, ts):
    hbuf[HALO:, :] = _rms(x_ref[...], gpre_ref[...]).astype(_BF16)
    pieces = []
    for c0 in range(0, _M_END, MXU_COLS):
        def main_piece(cs=slice(c0, c0 + MXU_COLS)):
            pm_ref[:, cs] = _dot(hbuf[HALO:, :], wmain_ref[:, cs])
        pieces.append(main_piece)
    for c0 in range(0, _C_END, MXU_COLS):
        def conv_piece(cs=slice(c0, c0 + MXU_COLS)):
            pc_ref[:, cs] = _dot(hbuf[...], wconv_ref[:, cs])
        pieces.append(conv_piece)

    def keep_halo():
        hbuf[0:HALO, :] = hbuf[ts:ts + HALO, :]
    pieces.append(keep_halo)
    return pieces


def _mixer_heads(pm, pcbuf, c1_ref, c2_ref, c1t_ref, c2t_ref, qn_ref, wqt_ref, kvn_ref, wk_ref,
                 wvt_ref, scw_ref, ssw_ref, ssb_ref, dtb_ref, aneg_ref, dskip_ref, snorm_ref,
                 qt_ref, k_ref, vt_ref, ycs_ref, ubuf, ybuf, state_ref, ts, pieces):
    def tick():
        if pieces:
            pieces.pop(0)()

    scale = MLA_QK_DIM ** -0.5 * math.log2(math.e)
    lane = lax.broadcasted_iota(jnp.int32, (1, LANES), 1)

    qn = _rms(pm[:, _M_CQ:_M_CQ + MLA_Q_LORA], qn_ref[...]).astype(_BF16)
    qt = _dot_nt(wqt_ref[...], qn)
    c1t = c1t_ref[...]
    c2t = c2t_ref[...]
    for hd in range(MLA_HEADS):
        a = qt[hd * HEAD_PAD:(hd + 1) * HEAD_PAD, :]
        a_swapped = jnp.concatenate([a[MLA_ROPE_DIM:, :], a[:MLA_ROPE_DIM, :]], axis=0)
        qt_ref[hd * HEAD_PAD:(hd + 1) * HEAD_PAD, :] = (
            (a * c1t + a_swapped * c2t) * scale).astype(_BF16)
        if hd == MLA_HEADS - 1:
            tick()
    kvn = _rms(pm[:, _M_CKV:_M_CKV + MLA_KV_LORA], kvn_ref[...]).astype(_BF16)
    vt_ref[...] = _dot_nt(wvt_ref[...], kvn).astype(_BF16)
    kfull = _dot(kvn, wk_ref[...])
    krdt = pm[:, _M_KRDT:_M_KRDT + LANES]
    kr = _rope_block(krdt, jnp.where(lane < MLA_NOPE_DIM, 0.0, c1_ref[...]), c2_ref[...])
    for hd in range(MLA_HEADS):
        sl = slice(hd * HEAD_PAD, (hd + 1) * HEAD_PAD)
        k_ref[:, sl] = (kfull[:, sl] + kr).astype(_BF16)

    ubuf[...] = pcbuf[:, _C_SCC:_C_SCC + SC_DIM] * pcbuf[:, _C_SCH:_C_SCH + SC_DIM]
    conv = _causal_taps(ubuf, scw_ref, slice(0, SC_DIM), slice(0, SC_DIM), ts)
    ycs_ref[:, 0:SC_DIM] = (pm[:, _M_SCB:_M_SCB + SC_DIM] * conv).astype(_BF16)

    xbc = _causal_taps(pcbuf, ssw_ref, slice(_C_XBC, _C_END), slice(0, SSD_CONV_DIM), ts)
    xbc = xbc + ssb_ref[...]
    xbc = xbc * jax.nn.sigmoid(xbc)
    z = pm[:, _M_Z:_M_Z + SSD_DIM]
    dt_raw = krdt + dtb_ref[...]
    dt = jnp.maximum(dt_raw, 0.0) + jnp.log1p(jnp.exp(-jnp.abs(dt_raw)))
    adt = dt * aneg_ref[...]
    tick()

    L = SSD_CHUNK
    nchunk = ts // L
    rows = lax.broadcasted_iota(jnp.int32, (L, L), 0)
    cols = lax.broadcasted_iota(jnp.int32, (L, L), 1)
    tri = rows >= cols
    head_lanes = lane < SSD_HEADS
    packed = jnp.where(head_lanes, adt[0:L, :], 0.0)
    for c in range(1, nchunk):
        packed = packed + pltpu.roll(jnp.where(head_lanes, adt[c * L:(c + 1) * L, :], 0.0),
                                     c * SSD_HEADS, 1)
    a_cs = _dot_f32(tri.astype(_F32), packed)
    a_cs_t = a_cs.T
    hpg = SSD_HEADS // SSD_GROUPS
    states = [state_ref[hh] for hh in range(SSD_HEADS)]
    for c in range(nchunk):
        r = slice(c * L, (c + 1) * L)
        xs_c = xbc[r, 0:SSD_DIM]
        dt_c = dt[r, :]
        for g in range(SSD_GROUPS):
            bg = xbc[r, SSD_DIM + g * SSD_STATE:SSD_DIM + (g + 1) * SSD_STATE]
            cg = xbc[r, SSD_DIM + SSD_BC_DIM + g * SSD_STATE:
                     SSD_DIM + SSD_BC_DIM + (g + 1) * SSD_STATE]
            sc = _dot_nt(cg.astype(_BF16), bg.astype(_BF16))
            tick()
            for hh in range(g * hpg, (g + 1) * hpg):
                hs = slice(hh * SSD_HEAD_DIM, (hh + 1) * SSD_HEAD_DIM)
                ln = c * SSD_HEADS + hh
                col = a_cs[:, ln:ln + 1]
                row = a_cs_t[ln:ln + 1, :]
                a_end = a_cs[L - 1:L, ln:ln + 1]
                dec = jnp.exp(jnp.where(tri, col - row, -jnp.inf))
                xs_h = xs_c[:, hs]
                xdt = (xs_h * dt_c[:, hh:hh + 1]).astype(_BF16)
                y = _dot((sc * dec).astype(_BF16), xdt)
                y = y + _dot((cg * jnp.exp(col)).astype(_BF16), states[hh].astype(_BF16))
                bd = (bg * jnp.exp(a_end - col)).astype(_BF16)
                states[hh] = states[hh] * jnp.exp(a_end) + _dot_tn(bd, xdt)
                ybuf[r, hs] = y + xs_h * dskip_ref[:, hs]
    for hh in range(SSD_HEADS):
        state_ref[hh] = states[hh]
    while pieces:
        tick()
    yg = ybuf[...] * (z * jax.nn.sigmoid(z))
    ycs_ref[:, SC_DIM:SC_DIM + SSD_DIM] = _rms(yg, snorm_ref[...]).astype(_BF16)


def _mixer_in_kernel(x_ref, c1_ref, c2_ref, c1t_ref, c2t_ref, gpre_ref, wmain_ref, wconv_ref,
                     qn_ref, wqt_ref, kvn_ref, wk_ref, wvt_ref, scw_ref, ssw_ref, ssb_ref, dtb_ref,
                     aneg_ref, dskip_ref, snorm_ref,
                     qt_ref, k_ref, vt_ref, ycs_ref,
                     hbuf, pm0, pm1, pc0, pc1, ubuf, ybuf, state_ref, *, ts, nt):
    i = pl.program_id(0)

    @pl.when(i == 0)
    def _():
        pm1[...] = jnp.zeros_like(pm1)
        pc1[...] = jnp.zeros_like(pc1)
        state_ref[...] = jnp.zeros_like(state_ref)

    @pl.when(lax.rem(i, nt) == 0)
    def _():
        hbuf[0:HALO, :] = jnp.zeros((HALO, D_MODEL), _BF16)

    @pl.when(lax.rem(i + nt - 1, nt) == 0)
    def _():
        state_ref[...] = jnp.zeros_like(state_ref)

    def step(pm_w, pc_w, pm_r, pc_r):
        pieces = _mixer_project(x_ref, gpre_ref, wmain_ref, wconv_ref, hbuf, pm_w, pc_w, ts)
        _mixer_heads(pm_r, pc_r, c1_ref, c2_ref, c1t_ref, c2t_ref, qn_ref, wqt_ref, kvn_ref,
                     wk_ref, wvt_ref, scw_ref, ssw_ref, ssb_ref, dtb_ref, aneg_ref, dskip_ref,
                     snorm_ref, qt_ref, k_ref, vt_ref, ycs_ref, ubuf, ybuf, state_ref, ts, pieces)

    @pl.when(lax.rem(i, 2) == 0)
    def _():
        step(pm0, pc0, pm1, pc1)

    @pl.when(lax.rem(i, 2) == 1)
    def _():
        step(pm1, pc1, pm0, pc0)


def _mixer_in(x, rope, p, l, ts):
    b, s, d = x.shape
    nt = s // ts
    ntiles = b * nt
    t_in = lambda i: jnp.minimum(i, ntiles - 1)
    t_out = lambda i: jnp.maximum(i - 1, 0)
    tile = lambda w, t: pl.BlockSpec((None, ts, w), lambda i: (t(i) // nt, t(i) % nt, 0))
    tile_t = lambda w, t: pl.BlockSpec((None, w, ts), lambda i: (t(i) // nt, 0, t(i) % nt))
    layer = lambda a: pl.BlockSpec((None,) + a.shape[1:], lambda i: (l, 0, 0))
    c1, c2, c1t, c2t = rope
    weights = [p["gpre"], p["wmain"], p["wconv"], p["qn"], p["wqt"], p["kvn"], p["wk"], p["wvt"],
               p["scw"], p["ssw"], p["ssb"], p["dtb"], p["aneg"], p["dskip"], p["snorm"]]
    out_shape = [
        jax.ShapeDtypeStruct((b, MLA_HEADS * HEAD_PAD, s), _BF16),
        jax.ShapeDtypeStruct((b, s, MLA_HEADS * HEAD_PAD), _BF16),
        jax.ShapeDtypeStruct((b, MLA_OUT, s), _BF16),
        jax.ShapeDtypeStruct((b, s, SC_DIM + SSD_DIM), _BF16),
    ]
    return pl.pallas_call(
        functools.partial(_mixer_in_kernel, ts=ts, nt=nt),
        grid=(ntiles + 1,),
        in_specs=[tile(d, t_in), tile(LANES, t_out), tile(LANES, t_out),
                  tile_t(LANES, t_out), tile_t(LANES, t_out)]
        + [layer(w) for w in weights],
        out_specs=[tile_t(MLA_HEADS * HEAD_PAD, t_out), tile(MLA_HEADS * HEAD_PAD, t_out),
                   tile_t(MLA_OUT, t_out), tile(SC_DIM + SSD_DIM, t_out)],
        out_shape=out_shape,
        scratch_shapes=[
            pltpu.VMEM((ts + HALO, D_MODEL), _BF16),
            pltpu.VMEM((ts, _M_END), _F32), pltpu.VMEM((ts, _M_END), _F32),
            pltpu.VMEM((ts + HALO, _C_END), _F32), pltpu.VMEM((ts + HALO, _C_END), _F32),
            pltpu.VMEM((ts + HALO, SC_DIM), _F32),
            pltpu.VMEM((ts, SSD_DIM), _F32),
            pltpu.VMEM((SSD_HEADS, SSD_STATE, SSD_HEAD_DIM), _F32),
        ],
        compiler_params=pltpu.CompilerParams(
            dimension_semantics=("arbitrary",), vmem_limit_bytes=VMEM_LIMIT),
        name="mixer_in",
    )(x, c1, c2, c1t, c2t, *weights)


def _attention_kernel(qt_ref, k_ref, vt_ref, o_ref, s0_ref, s1_ref, m_ref, acc_ref, *, t):
    qi = pl.program_id(1)
    half = t // 2
    tri = (lax.broadcasted_iota(jnp.int32, (half, half), 0)
           <= lax.broadcasted_iota(jnp.int32, (half, half), 1))
    sbufs = (s0_ref, s1_ref)
    hsl = lambda hd: slice(hd * HEAD_PAD, (hd + 1) * HEAD_PAD)
    vsl = lambda hd: slice(hd * MLA_V_DIM, (hd + 1) * MLA_V_DIM)

    def online(st, vt, m, acc):
        m_new = jnp.maximum(m, jnp.max(st, axis=0, keepdims=True))
        alpha = jnp.exp2(m - m_new)
        pt = jnp.exp2(st - m_new).astype(_BF16)
        lhs = jnp.concatenate([vt, jnp.ones((ATT_SUM_ROWS, vt.shape[1]), _BF16)], axis=0)
        return m_new, alpha * acc + _dot(lhs, pt)

    def scores(hd, j):
        start = pl.multiple_of(j * t, t)
        sbufs[hd % 2][...] = _dot(k_ref[pl.ds(start, t), hsl(hd)], qt_ref[hsl(hd), :])

    def softmax_pv(hd, j):
        start = pl.multiple_of(j * t, t)
        m, acc = online(sbufs[hd % 2][...], vt_ref[vsl(hd), pl.ds(start, t)],
                        m_ref[hd], acc_ref[hd])
        m_ref[hd] = m
        acc_ref[hd] = acc

    def scores_diag(hd):
        lo = pl.multiple_of(qi * t, t)
        hi = pl.multiple_of(qi * t + half, half)
        sref = sbufs[hd % 2]
        sref[0:half, :] = _dot(k_ref[pl.ds(lo, half), hsl(hd)], qt_ref[hsl(hd), :])
        sref[half:, half:] = _dot(k_ref[pl.ds(hi, half), hsl(hd)], qt_ref[hsl(hd), half:])

    def softmax_pv_diag(hd):
        lo = pl.multiple_of(qi * t, t)
        hi = pl.multiple_of(qi * t + half, half)
        sref = sbufs[hd % 2]
        st = jnp.concatenate([jnp.where(tri, sref[0:half, 0:half], -jnp.inf),
                              sref[0:half, half:]], axis=1)
        m, acc = online(st, vt_ref[vsl(hd), pl.ds(lo, half)], m_ref[hd], acc_ref[hd])
        st = jnp.where(tri, sref[half:, half:], -jnp.inf)
        _, acc_hi = online(st, vt_ref[vsl(hd), pl.ds(hi, half)], m[:, half:], acc[:, half:])
        acc = jnp.concatenate([acc[:, :half], acc_hi], axis=1)
        o = acc[0:MLA_V_DIM, :] / acc[MLA_V_DIM:MLA_V_DIM + 1, :]
        o_ref[:, vsl(hd)] = o.T.astype(_BF16)

    m_ref[...] = jnp.full(m_ref.shape, -jnp.inf, _F32)
    acc_ref[...] = jnp.zeros(acc_ref.shape, _F32)

    scores(0, 0)

    def body(j, carry):
        for hd in range(MLA_HEADS):
            if hd + 1 < MLA_HEADS:
                scores(hd + 1, j)
            else:
                scores(0, j + 1)
            softmax_pv(hd, j)
        return carry

    lax.fori_loop(0, qi, body, 0)
    for hd in range(MLA_HEADS):
        if hd + 1 < MLA_HEADS:
            scores_diag(hd + 1)
        softmax_pv_diag(hd)


def _attention(qt, k, vt, t):
    b, s, _ = k.shape
    return pl.pallas_call(
        functools.partial(_attention_kernel, t=t),
        grid=(b, s // t),
        in_specs=[
            pl.BlockSpec((None, MLA_HEADS * HEAD_PAD, t), lambda bi, qi: (bi, 0, qi)),
            pl.BlockSpec((None, s, MLA_HEADS * HEAD_PAD), lambda bi, qi: (bi, 0, 0)),
            pl.BlockSpec((None, MLA_OUT, s), lambda bi, qi: (bi, 0, 0)),
        ],
        out_specs=pl.BlockSpec((None, t, MLA_OUT), lambda bi, qi: (bi, qi, 0)),
        out_shape=jax.ShapeDtypeStruct((b, s, MLA_OUT), _BF16),
        scratch_shapes=[
            pltpu.VMEM((t, t), _F32), pltpu.VMEM((t, t), _F32),
            pltpu.VMEM((MLA_HEADS, 1, t), _F32),
            pltpu.VMEM((MLA_HEADS, MLA_V_DIM + ATT_SUM_ROWS, t), _F32),
        ],
        compiler_params=pltpu.CompilerParams(
            dimension_semantics=("arbitrary", "arbitrary"), vmem_limit_bytes=VMEM_LIMIT),
        name="attention",
    )(qt, k, vt)


def _out_ffn_kernel(x_ref, att_ref, ycs_ref, woa_ref, woc_ref, gpost_ref, gfpre_ref, wg_ref,
                    wu_ref, cwg_ref, cwu_ref, cbg_ref, cbu_ref, wd_ref, gfpost_ref,
                    o_ref, hbuf, gbuf, ubuf, *, tm):
    si = pl.program_id(1)

    @pl.when(si == 0)
    def _():
        hbuf[0:HALO, :] = jnp.zeros((HALO, D_MODEL), _BF16)

    mixed = _dot(att_ref[...], woa_ref[...]) + _dot(ycs_ref[...], woc_ref[...])
    x1 = x_ref[...] + _rms(mixed, gpost_ref[...])
    h = _rms(x1, gfpre_ref[...]).astype(_BF16)
    hbuf[HALO:, :] = h
    acc = jnp.zeros((tm, D_MODEL), _F32)
    off = 0
    for fc in FFN_CHUNKS:
        cs = slice(off, off + fc)
        off += fc
        gbuf[:, 0:fc] = _dot(hbuf[...], wg_ref[:, cs])
        ubuf[:, 0:fc] = _dot(hbuf[...], wu_ref[:, cs])
        g = _causal_taps(gbuf, cwg_ref, slice(0, fc), cs, tm) + cbg_ref[:, cs]
        u = _causal_taps(ubuf, cwu_ref, slice(0, fc), cs, tm) + cbu_ref[:, cs]
        act = (g * jax.nn.sigmoid(g) * u).astype(_BF16)
        acc = acc + _dot(act, wd_ref[cs, :])
    hbuf[0:HALO, :] = h[tm - HALO:, :]
    o_ref[...] = x1 + _rms(acc, gfpost_ref[...])


def _out_ffn(x, att, ycs, p, l, tm):
    b, s, d = x.shape
    tile = lambda w: pl.BlockSpec((None, tm, w), lambda bi, si: (bi, si, 0))

    def layer(a, axis=None, part=0):
        shape, idx = list(a.shape[1:]), [0, 0]
        if axis is not None:
            shape[axis - 1] //= 2
            idx[axis - 1] = part
        return pl.BlockSpec((None, *shape), lambda bi, si: (l, *idx), pipeline_mode=pl.Buffered(1))

    operands = [
        (p["wo"], layer(p["wo"], 1, 0)), (p["wo"], layer(p["wo"], 1, 1)),
        (p["gpost"], layer(p["gpost"])), (p["gfpre"], layer(p["gfpre"])),
        (p["wup"], layer(p["wup"], 2, 0)), (p["wup"], layer(p["wup"], 2, 1)),
        (p["cw"], layer(p["cw"], 2, 0)), (p["cw"], layer(p["cw"], 2, 1)),
        (p["cb"], layer(p["cb"], 2, 0)), (p["cb"], layer(p["cb"], 2, 1)),
        (p["wd"], layer(p["wd"])), (p["gfpost"], layer(p["gfpost"])),
    ]
    weights = [a for a, _ in operands]
    return pl.pallas_call(
        functools.partial(_out_ffn_kernel, tm=tm),
        grid=(b, s // tm),
        in_specs=[tile(d), tile(MLA_OUT), tile(SC_DIM + SSD_DIM)] + [sp for _, sp in operands],
        out_specs=tile(d),
        out_shape=jax.ShapeDtypeStruct((b, s, d), _F32),
        scratch_shapes=[
            pltpu.VMEM((tm + HALO, D_MODEL), _BF16),
            pltpu.VMEM((tm + HALO, max(FFN_CHUNKS)), _F32),
            pltpu.VMEM((tm + HALO, max(FFN_CHUNKS)), _F32),
        ],
        compiler_params=pltpu.CompilerParams(
            dimension_semantics=("arbitrary", "arbitrary"), vmem_limit_bytes=VMEM_LIMIT),
        name="out_ffn",
    )(x, att, ycs, *weights)


def _swap_halves(w):
    half = w.shape[-1] // 2
    return jnp.concatenate([w[..., half:], w[..., :half]], axis=-1)


def _pack_params(norm_mix_pre, norm_mix_post, norm_ffn_pre, norm_ffn_post, w_in, mla_q_norm,
                 mla_w_q_up, mla_kv_norm, mla_w_kv_up, sc_conv_w, ssd_conv_w, ssd_conv_b,
                 ssd_dt_bias, ssd_a_log, ssd_d, ssd_norm, w_out, ffn_w_up, ffn_conv_w, ffn_conv_b,
                 ffn_w_down):
    depth = w_in.shape[0]
    f32 = lambda a: a.astype(_F32)
    row = lambda a: f32(a)[:, None, :]
    splits = np.cumsum(IN_WIDTHS)[:-1]
    w_cq, w_ckv, w_kr, w_scb, w_scc, w_sch, w_ssd = jnp.split(w_in, splits, axis=-1)
    w_z = w_ssd[..., :SSD_DIM]
    w_xbc = w_ssd[..., SSD_DIM:SSD_DIM + SSD_CONV_DIM]
    w_dt = w_ssd[..., SSD_DIM + SSD_CONV_DIM:]
    zeros = jnp.zeros((depth, D_MODEL, MLA_NOPE_DIM - SSD_HEADS), w_in.dtype)
    wmain = jnp.concatenate(
        [w_cq, w_ckv, w_dt, zeros, w_kr, _swap_halves(w_kr), w_scb, w_z], axis=-1)
    wconv = jnp.concatenate([w_scc, w_sch, w_xbc], axis=-1)

    wq4 = mla_w_q_up.reshape(depth, MLA_Q_LORA, MLA_HEADS, MLA_QK_DIM)
    wq = jnp.concatenate([wq4, _swap_halves(wq4[..., MLA_NOPE_DIM:])], axis=-1)
    wq = wq.reshape(depth, MLA_Q_LORA, MLA_HEADS * HEAD_PAD)
    wkv4 = mla_w_kv_up.reshape(depth, MLA_KV_LORA, MLA_HEADS, MLA_NOPE_DIM + MLA_V_DIM)
    wk = jnp.pad(wkv4[..., :MLA_NOPE_DIM], ((0, 0),) * 3 + ((0, HEAD_PAD - MLA_NOPE_DIM),))
    wk = wk.reshape(depth, MLA_KV_LORA, MLA_HEADS * HEAD_PAD)
    wvt = jnp.swapaxes(wkv4[..., MLA_NOPE_DIM:].reshape(depth, MLA_KV_LORA, MLA_OUT), 1, 2)

    pad_heads = lambda a: jnp.pad(f32(a), ((0, 0), (0, LANES - SSD_HEADS)))[:, None, :]
    return {
        "gpre": row(norm_mix_pre), "wmain": wmain.astype(_BF16), "wconv": wconv.astype(_BF16),
        "qn": row(mla_q_norm), "wqt": jnp.swapaxes(wq, 1, 2).astype(_BF16),
        "kvn": row(mla_kv_norm), "wk": wk.astype(_BF16), "wvt": wvt.astype(_BF16),
        "scw": f32(sc_conv_w), "ssw": f32(ssd_conv_w),
        "ssb": row(ssd_conv_b), "dtb": pad_heads(ssd_dt_bias),
        "aneg": -jnp.exp(pad_heads(ssd_a_log)),
        "dskip": jnp.repeat(f32(ssd_d), SSD_HEAD_DIM, axis=-1)[:, None, :],
        "snorm": row(ssd_norm),
        "wo": w_out.astype(_BF16), "gpost": row(norm_mix_post), "gfpre": row(norm_ffn_pre),
        "wup": ffn_w_up.astype(_BF16), "cw": f32(ffn_conv_w), "cb": row(ffn_conv_b),
        "wd": ffn_w_down.astype(_BF16), "gfpost": row(norm_ffn_post),
    }


def _rope_tables(positions):
    inv_freq = 1.0 / (ROPE_THETA ** (jnp.arange(0, MLA_ROPE_DIM, 2, dtype=_F32) / MLA_ROPE_DIM))
    ang = positions.astype(_F32)[..., None] * inv_freq
    cos = jnp.cos(ang)
    sin = jnp.sin(ang)
    lead = positions.shape + (MLA_NOPE_DIM,)
    tail = positions.shape + (LANES - MLA_NOPE_DIM - MLA_ROPE_DIM,)
    c1 = jnp.concatenate([jnp.ones(lead, _F32), cos, cos, jnp.zeros(tail, _F32)], axis=-1)
    c2 = jnp.concatenate([jnp.zeros(lead, _F32), -sin, sin, jnp.zeros(tail, _F32)], axis=-1)
    return c1, c2, jnp.swapaxes(c1, 1, 2), jnp.swapaxes(c2, 1, 2)


def kernel(x, positions, norm_mix_pre, norm_mix_post, norm_ffn_pre, norm_ffn_post, w_in, mla_q_norm, mla_w_q_up, mla_kv_norm, mla_w_kv_up, sc_conv_w, ssd_conv_w, ssd_conv_b, ssd_dt_bias, ssd_a_log, ssd_d, ssd_norm, w_out, ffn_w_up, ffn_conv_w, ffn_conv_b, ffn_w_down):
    depth = w_in.shape[0]
    s = x.shape[1]
    ts = min(SEQ_TILE_MIX, s)
    ta = min(SEQ_TILE_ATT, s)
    tm = min(SEQ_TILE_FFN, s)
    assert s % ts == 0 and s % ta == 0 and s % tm == 0 and ts % SSD_CHUNK == 0
    rope = _rope_tables(positions)
    p = _pack_params(norm_mix_pre, norm_mix_post, norm_ffn_pre, norm_ffn_post, w_in, mla_q_norm,
                     mla_w_q_up, mla_kv_norm, mla_w_kv_up, sc_conv_w, ssd_conv_w, ssd_conv_b,
                     ssd_dt_bias, ssd_a_log, ssd_d, ssd_norm, w_out, ffn_w_up, ffn_conv_w,
                     ffn_conv_b, ffn_w_down)
    for l in range(depth):
        qt, k, vt, ycs = _mixer_in(x, rope, p, l, ts)
        att = _attention(qt, k, vt, ta)
        x = _out_ffn(x, att, ycs, p, l, tm)
    return x
```

```python
import functools
import math

import jax
import jax.numpy as jnp
import numpy as np
from jax import lax
from jax.experimental import pallas as pl
from jax.experimental.pallas import tpu as pltpu

D_MODEL = 1024
MLA_HEADS = 8
MLA_Q_LORA = 256
MLA_KV_LORA = 128
MLA_NOPE_DIM = 64
MLA_ROPE_DIM = 32
MLA_V_DIM = 64
ROPE_THETA = 10000.0
SC_DIM = 256
SC_WIDTH = 3
SSD_HEADS = 4
SSD_HEAD_DIM = 64
SSD_GROUPS = 2
SSD_STATE = 128
SSD_CONV_WIDTH = 4
SSD_CHUNK = 128
FFN_DIM = 2816
FFN_CONV_WIDTH = 3
NORM_EPS = 1e-6

MLA_QK_DIM = MLA_NOPE_DIM + MLA_ROPE_DIM
MLA_OUT = MLA_HEADS * MLA_V_DIM
SSD_DIM = SSD_HEADS * SSD_HEAD_DIM
SSD_BC_DIM = SSD_GROUPS * SSD_STATE
SSD_CONV_DIM = SSD_DIM + 2 * SSD_BC_DIM
SSD_IN = SSD_DIM + SSD_CONV_DIM + SSD_HEADS
IN_WIDTHS = (MLA_Q_LORA, MLA_KV_LORA, MLA_ROPE_DIM, SC_DIM, SC_DIM, SC_DIM, SSD_IN)
D_IN = sum(IN_WIDTHS)

LANES = 128
MXU_COLS = 256
HEAD_PAD = LANES
HALO = 16

_M_CQ = 0
_M_CKV = _M_CQ + MLA_Q_LORA
_M_KRDT = _M_CKV + MLA_KV_LORA
_M_SCB = _M_KRDT + LANES
_M_Z = _M_SCB + SC_DIM
_M_END = _M_Z + SSD_DIM
_C_SCC = 0
_C_SCH = _C_SCC + SC_DIM
_C_XBC = _C_SCH + SC_DIM
_C_END = _C_XBC + SSD_CONV_DIM

SEQ_TILE_MIX = 512
SEQ_TILE_ATT = 512
ATT_SUM_ROWS = 16
SEQ_TILE_FFN = 512
FFN_CHUNKS = (FFN_DIM,)
VMEM_LIMIT = 56 * 1024 * 1024

assert _M_END % MXU_COLS == 0 and _C_END % MXU_COLS == 0
assert SSD_HEADS == 2 * SSD_GROUPS and 2 * SSD_HEAD_DIM == LANES
assert sum(FFN_CHUNKS) == FFN_DIM and all(c % MXU_COLS == 0 for c in FFN_CHUNKS)

_F32 = jnp.float32
_BF16 = jnp.bfloat16


def _rms(x, w):
    return x * lax.rsqrt(jnp.mean(x * x, axis=-1, keepdims=True) + NORM_EPS) * w


def _dot(a, b):
    return jnp.dot(a, b, preferred_element_type=_F32)


def _dot_nt(a, b):
    return lax.dot_general(a, b, (((1,), (1,)), ((), ())), preferred_element_type=_F32)


def _dot_tn(a, b):
    return lax.dot_general(a, b, (((0,), (0,)), ((), ())), preferred_element_type=_F32)


def _dot_f32(a, b):
    return jnp.dot(a, b, preferred_element_type=_F32, precision=lax.Precision.HIGHEST)


def _causal_taps(buf, w_ref, cols, wcols, rows):
    width = w_ref.shape[0]
    out = buf[HALO:HALO + rows, cols] * w_ref[width - 1:width, wcols]
    for i in range(width - 1):
        o = HALO - width + 1 + i
        out = out + buf[o:o + rows, cols] * w_ref[i:i + 1, wcols]
    return out


def _rope_block(a, c1, c2):
    return a * c1 + pltpu.roll(a, LANES - MLA_ROPE_DIM, 1) * c2


def _mixer_project(x_ref, gpre_ref, wmain_ref, wconv_ref, hbuf, pm_ref, pc_ref, ts):
    hbuf[HALO:, :] = _rms(x_ref[...], gpre_ref[...]).astype(_BF16)
    pieces = []
    for c0 in range(0, _M_END, MXU_COLS):
        def main_piece(cs=slice(c0, c0 + MXU_COLS)):
            pm_ref[:, cs] = _dot(hbuf[HALO:, :], wmain_ref[:, cs])
        pieces.append(main_piece)
    for c0 in range(0, _C_END, MXU_COLS):
        def conv_piece(cs=slice(c0, c0 + MXU_COLS)):
            pc_ref[:, cs] = _dot(hbuf[...], wconv_ref[:, cs])
        pieces.append(conv_piece)

    def keep_halo():
        hbuf[0:HALO, :] = hbuf[ts:ts + HALO, :]
    pieces.append(keep_halo)
    return pieces


def _mixer_heads(pm, pcbuf, c1_ref, c2_ref, c1t_ref, c2t_ref, qn_ref, wqt_ref, kvn_ref, wk_ref,
                 wvt_ref, scw_ref, ssw_ref, ssb_ref, dtb_ref, aneg_ref, dskip_ref, snorm_ref,
                 qt_ref, k_ref, vt_ref, ycs_ref, ubuf, ybuf, state_ref, ts, pieces):
    def tick():
        if pieces:
            pieces.pop(0)()

    scale = MLA_QK_DIM ** -0.5 * math.log2(math.e)
    lane = lax.broadcasted_iota(jnp.int32, (1, LANES), 1)

    qn = _rms(pm[:, _M_CQ:_M_CQ + MLA_Q_LORA], qn_ref[...]).astype(_BF16)
    qt = _dot_nt(wqt_ref[...], qn)
    c1t = c1t_ref[...]
    c2t = c2t_ref[...]
    for hd in range(MLA_HEADS):
        a = qt[hd * HEAD_PAD:(hd + 1) * HEAD_PAD, :]
        a_swapped = jnp.concatenate([a[MLA_ROPE_DIM:, :], a[:MLA_ROPE_DIM, :]], axis=0)
        qt_ref[hd * HEAD_PAD:(hd + 1) * HEAD_PAD, :] = (
            (a * c1t + a_swapped * c2t) * scale).astype(_BF16)
        if hd == MLA_HEADS - 1:
            tick()
    kvn = _rms(pm[:, _M_CKV:_M_CKV + MLA_KV_LORA], kvn_ref[...]).astype(_BF16)
    vt_ref[...] = _dot_nt(wvt_ref[...], kvn).astype(_BF16)
    kfull = _dot(kvn, wk_ref[...])
    krdt = pm[:, _M_KRDT:_M_KRDT + LANES]
    kr = _rope_block(krdt, jnp.where(lane < MLA_NOPE_DIM, 0.0, c1_ref[...]), c2_ref[...])
    for hd in range(MLA_HEADS):
        sl = slice(hd * HEAD_PAD, (hd + 1) * HEAD_PAD)
        k_ref[:, sl] = (kfull[:, sl] + kr).astype(_BF16)

    ubuf[...] = pcbuf[:, _C_SCC:_C_SCC + SC_DIM] * pcbuf[:, _C_SCH:_C_SCH + SC_DIM]
    conv = _causal_taps(ubuf, scw_ref, slice(0, SC_DIM), slice(0, SC_DIM), ts)
    ycs_ref[:, 0:SC_DIM] = (pm[:, _M_SCB:_M_SCB + SC_DIM] * conv).astype(_BF16)

    xbc = _causal_taps(pcbuf, ssw_ref, slice(_C_XBC, _C_END), slice(0, SSD_CONV_DIM), ts)
    xbc = xbc + ssb_ref[...]
    xbc = xbc * jax.nn.sigmoid(xbc)
    z = pm[:, _M_Z:_M_Z + SSD_DIM]
    dt_raw = krdt + dtb_ref[...]
    dt = jnp.maximum(dt_raw, 0.0) + jnp.log1p(jnp.exp(-jnp.abs(dt_raw)))
    adt = dt * aneg_ref[...]
    tick()

    L = SSD_CHUNK
    nchunk = ts // L
    rows = lax.broadcasted_iota(jnp.int32, (L, L), 0)
    cols = lax.broadcasted_iota(jnp.int32, (L, L), 1)
    tri = rows >= cols
    head_lanes = lane < SSD_HEADS
    packed = jnp.where(head_lanes, adt[0:L, :], 0.0)
    for c in range(1, nchunk):
        packed = packed + pltpu.roll(jnp.where(head_lanes, adt[c * L:(c + 1) * L, :], 0.0),
                                     c * SSD_HEADS, 1)
    a_cs = _dot_f32(tri.astype(_F32), packed)
    a_cs_t = a_cs.T
    n = SSD_STATE
    first = lane < SSD_HEAD_DIM
    row2 = lax.broadcasted_iota(jnp.int32, (2 * n, LANES), 0)
    lane2 = lax.broadcasted_iota(jnp.int32, (2 * n, LANES), 1)
    diag_blocks = (row2 < n) == (lane2 < SSD_HEAD_DIM)
    states = [state_ref[g] for g in range(SSD_GROUPS)]
    for c in range(nchunk):
        r = slice(c * L, (c + 1) * L)
        dt_c = dt[r, :]
        for g in range(SSD_GROUPS):
            gsl = slice(g * LANES, (g + 1) * LANES)
            xs_g = xbc[r, gsl]
            bg = xbc[r, SSD_DIM + g * n:SSD_DIM + (g + 1) * n]
            cg = xbc[r, SSD_DIM + SSD_BC_DIM + g * n:SSD_DIM + SSD_BC_DIM + (g + 1) * n]
            sc = _dot_nt(cg.astype(_BF16), bg.astype(_BF16))
            tick()
            h0 = g * 2
            ln = c * SSD_HEADS + h0
            col = [a_cs[:, ln + k:ln + k + 1] for k in range(2)]
            row = [a_cs_t[ln + k:ln + k + 1, :] for k in range(2)]
            end = [a_cs[L - 1:L, ln + k:ln + k + 1] for k in range(2)]
            xdt = xs_g * jnp.where(first, dt_c[:, h0:h0 + 1], dt_c[:, h0 + 1:h0 + 2])
            xdt_blocks = jnp.concatenate(
                [jnp.where(first, xdt, 0.0).astype(_BF16), jnp.where(first, 0.0, xdt).astype(_BF16)],
                axis=0)
            dec = [jnp.exp(jnp.where(tri, col[k] - row[k], -jnp.inf)) for k in range(2)]
            m_g = jnp.concatenate([(sc * dec[k]).astype(_BF16) for k in range(2)], axis=1)
            y = _dot(m_g, xdt_blocks)
            cge = jnp.concatenate([(cg * jnp.exp(col[k])).astype(_BF16) for k in range(2)], axis=1)
            y = y + _dot(cge, states[g].astype(_BF16))
            bd = jnp.concatenate(
                [(bg * jnp.exp(end[k] - col[k])).astype(_BF16) for k in range(2)], axis=1)
            upd = _dot_tn(bd, xdt.astype(_BF16))
            decay = jnp.where(row2 < n, jnp.exp(end[0]), jnp.exp(end[1]))
            states[g] = states[g] * decay + jnp.where(diag_blocks, upd, 0.0)
            ybuf[r, gsl] = y + xs_g * dskip_ref[:, gsl]
    for g in range(SSD_GROUPS):
        state_ref[g] = states[g]
    while pieces:
        tick()
    yg = ybuf[...] * (z * jax.nn.sigmoid(z))
    ycs_ref[:, SC_DIM:SC_DIM + SSD_DIM] = _rms(yg, snorm_ref[...]).astype(_BF16)


def _mixer_in_kernel(x_ref, c1_ref, c2_ref, c1t_ref, c2t_ref, gpre_ref, wmain_ref, wconv_ref,
                     qn_ref, wqt_ref, kvn_ref, wk_ref, wvt_ref, scw_ref, ssw_ref, ssb_ref, dtb_ref,
                     aneg_ref, dskip_ref, snorm_ref,
                     qt_ref, k_ref, vt_ref, ycs_ref,
                     hbuf, pm0, pm1, pc0, pc1, ubuf, ybuf, state_ref, *, ts, nt):
    i = pl.program_id(0)

    @pl.when(i == 0)
    def _():
        pm1[...] = jnp.zeros_like(pm1)
        pc1[...] = jnp.zeros_like(pc1)
        state_ref[...] = jnp.zeros_like(state_ref)

    @pl.when(lax.rem(i, nt) == 0)
    def _():
        hbuf[0:HALO, :] = jnp.zeros((HALO, D_MODEL), _BF16)

    @pl.when(lax.rem(i + nt - 1, nt) == 0)
    def _():
        state_ref[...] = jnp.zeros_like(state_ref)

    def step(pm_w, pc_w, pm_r, pc_r):
        pieces = _mixer_project(x_ref, gpre_ref, wmain_ref, wconv_ref, hbuf, pm_w, pc_w, ts)
        _mixer_heads(pm_r, pc_r, c1_ref, c2_ref, c1t_ref, c2t_ref, qn_ref, wqt_ref, kvn_ref,
                     wk_ref, wvt_ref, scw_ref, ssw_ref, ssb_ref, dtb_ref, aneg_ref, dskip_ref,
                     snorm_ref, qt_ref, k_ref, vt_ref, ycs_ref, ubuf, ybuf, state_ref, ts, pieces)

    @pl.when(lax.rem(i, 2) == 0)
    def _():
        step(pm0, pc0, pm1, pc1)

    @pl.when(lax.rem(i, 2) == 1)
    def _():
        step(pm1, pc1, pm0, pc0)


def _mixer_in(x, rope, p, l, ts):
    b, s, d = x.shape
    nt = s // ts
    ntiles = b * nt
    t_in = lambda i: jnp.minimum(i, ntiles - 1)
    t_out = lambda i: jnp.maximum(i - 1, 0)
    tile = lambda w, t: pl.BlockSpec((None, ts, w), lambda i: (t(i) // nt, t(i) % nt, 0))
    tile_t = lambda w, t: pl.BlockSpec((None, w, ts), lambda i: (t(i) // nt, 0, t(i) % nt))
    layer = lambda a: pl.BlockSpec((None,) + a.shape[1:], lambda i: (l, 0, 0))
    c1, c2, c1t, c2t = rope
    weights = [p["gpre"], p["wmain"], p["wconv"], p["qn"], p["wqt"], p["kvn"], p["wk"], p["wvt"],
               p["scw"], p["ssw"], p["ssb"], p["dtb"], p["aneg"], p["dskip"], p["snorm"]]
    out_shape = [
        jax.ShapeDtypeStruct((b, MLA_HEADS * HEAD_PAD, s), _BF16),
        jax.ShapeDtypeStruct((b, s, MLA_HEADS * HEAD_PAD), _BF16),
        jax.ShapeDtypeStruct((b, MLA_OUT, s), _BF16),
        jax.ShapeDtypeStruct((b, s, SC_DIM + SSD_DIM), _BF16),
    ]
    return pl.pallas_call(
        functools.partial(_mixer_in_kernel, ts=ts, nt=nt),
        grid=(ntiles + 1,),
        in_specs=[tile(d, t_in), tile(LANES, t_out), tile(LANES, t_out),
                  tile_t(LANES, t_out), tile_t(LANES, t_out)]
        + [layer(w) for w in weights],
        out_specs=[tile_t(MLA_HEADS * HEAD_PAD, t_out), tile(MLA_HEADS * HEAD_PAD, t_out),
                   tile_t(MLA_OUT, t_out), tile(SC_DIM + SSD_DIM, t_out)],
        out_shape=out_shape,
        scratch_shapes=[
            pltpu.VMEM((ts + HALO, D_MODEL), _BF16),
            pltpu.VMEM((ts, _M_END), _F32), pltpu.VMEM((ts, _M_END), _F32),
            pltpu.VMEM((ts + HALO, _C_END), _F32), pltpu.VMEM((ts + HALO, _C_END), _F32),
            pltpu.VMEM((ts + HALO, SC_DIM), _F32),
            pltpu.VMEM((ts, SSD_DIM), _F32),
            pltpu.VMEM((SSD_GROUPS, 2 * SSD_STATE, LANES), _F32),
        ],
        compiler_params=pltpu.CompilerParams(
            dimension_semantics=("arbitrary",), vmem_limit_bytes=VMEM_LIMIT),
        name="mixer_in",
    )(x, c1, c2, c1t, c2t, *weights)


def _attention_kernel(qt_ref, k_ref, vt_ref, o_ref, s0_ref, s1_ref, m_ref, acc_ref, *, t):
    qi = pl.program_id(1)
    half = t // 2
    tri = (lax.broadcasted_iota(jnp.int32, (half, half), 0)
           <= lax.broadcasted_iota(jnp.int32, (half, half), 1))
    sbufs = (s0_ref, s1_ref)
    hsl = lambda hd: slice(hd * HEAD_PAD, (hd + 1) * HEAD_PAD)
    vsl = lambda hd: slice(hd * MLA_V_DIM, (hd + 1) * MLA_V_DIM)

    def online(st, vt, m, acc):
        m_new = jnp.maximum(m, jnp.max(st, axis=0, keepdims=True))
        alpha = jnp.exp2(m - m_new)
        pt = jnp.exp2(st - m_new).astype(_BF16)
        lhs = jnp.concatenate([vt, jnp.ones((ATT_SUM_ROWS, vt.shape[1]), _BF16)], axis=0)
        return m_new, alpha * acc + _dot(lhs, pt)

    def scores(hd, j):
        start = pl.multiple_of(j * t, t)
        sbufs[hd % 2][...] = _dot(k_ref[pl.ds(start, t), hsl(hd)], qt_ref[hsl(hd), :])

    def softmax_pv(hd, j):
        start = pl.multiple_of(j * t, t)
        m, acc = online(sbufs[hd % 2][...], vt_ref[vsl(hd), pl.ds(start, t)],
                        m_ref[hd], acc_ref[hd])
        m_ref[hd] = m
        acc_ref[hd] = acc

    def scores_diag(hd):
        lo = pl.multiple_of(qi * t, t)
        hi = pl.multiple_of(qi * t + half, half)
        sref = sbufs[hd % 2]
        sref[0:half, :] = _dot(k_ref[pl.ds(lo, half), hsl(hd)], qt_ref[hsl(hd), :])
        sref[half:, half:] = _dot(k_ref[pl.ds(hi, half), hsl(hd)], qt_ref[hsl(hd), half:])

    def softmax_pv_diag(hd):
        lo = pl.multiple_of(qi * t, t)
        hi = pl.multiple_of(qi * t + half, half)
        sref = sbufs[hd % 2]
        st = jnp.concatenate([jnp.where(tri, sref[0:half, 0:half], -jnp.inf),
                              sref[0:half, half:]], axis=1)
        m, acc = online(st, vt_ref[vsl(hd), pl.ds(lo, half)], m_ref[hd], acc_ref[hd])
        st = jnp.where(tri, sref[half:, half:], -jnp.inf)
        _, acc_hi = online(st, vt_ref[vsl(hd), pl.ds(hi, half)], m[:, half:], acc[:, half:])
        acc = jnp.concatenate([acc[:, :half], acc_hi], axis=1)
        o = acc[0:MLA_V_DIM, :] / acc[MLA_V_DIM:MLA_V_DIM + 1, :]
        o_ref[:, vsl(hd)] = o.T.astype(_BF16)

    m_ref[...] = jnp.full(m_ref.shape, -jnp.inf, _F32)
    acc_ref[...] = jnp.zeros(acc_ref.shape, _F32)

    scores(0, 0)

    def body(j, carry):
        for hd in range(MLA_HEADS):
            if hd + 1 < MLA_HEADS:
                scores(hd + 1, j)
            else:
                scores(0, j + 1)
            softmax_pv(hd, j)
        return carry

    lax.fori_loop(0, qi, body, 0)
    for hd in range(MLA_HEADS):
        if hd + 1 < MLA_HEADS:
            scores_diag(hd + 1)
        softmax_pv_diag(hd)


def _attention(qt, k, vt, t):
    b, s, _ = k.shape
    return pl.pallas_call(
        functools.partial(_attention_kernel, t=t),
        grid=(b, s // t),
        in_specs=[
            pl.BlockSpec((None, MLA_HEADS * HEAD_PAD, t), lambda bi, qi: (bi, 0, qi)),
            pl.BlockSpec((None, s, MLA_HEADS * HEAD_PAD), lambda bi, qi: (bi, 0, 0)),
            pl.BlockSpec((None, MLA_OUT, s), lambda bi, qi: (bi, 0, 0)),
        ],
        out_specs=pl.BlockSpec((None, t, MLA_OUT), lambda bi, qi: (bi, qi, 0)),
        out_shape=jax.ShapeDtypeStruct((b, s, MLA_OUT), _BF16),
        scratch_shapes=[
            pltpu.VMEM((t, t), _F32), pltpu.VMEM((t, t), _F32),
            pltpu.VMEM((MLA_HEADS, 1, t), _F32),
            pltpu.VMEM((MLA_HEADS, MLA_V_DIM + ATT_SUM_ROWS, t), _F32),
        ],
        compiler_params=pltpu.CompilerParams(
            dimension_semantics=("arbitrary", "arbitrary"), vmem_limit_bytes=VMEM_LIMIT),
        name="attention",
    )(qt, k, vt)


def _out_ffn_kernel(x_ref, att_ref, ycs_ref, woa_ref, woc_ref, gpost_ref, gfpre_ref, wg_ref,
                    wu_ref, cwg_ref, cwu_ref, cbg_ref, cbu_ref, wd_ref, gfpost_ref,
                    o_ref, hbuf, gbuf, ubuf, *, tm):
    si = pl.program_id(1)

    @pl.when(si == 0)
    def _():
        hbuf[0:HALO, :] = jnp.zeros((HALO, D_MODEL), _BF16)

    mixed = _dot(att_ref[...], woa_ref[...]) + _dot(ycs_ref[...], woc_ref[...])
    x1 = x_ref[...] + _rms(mixed, gpost_ref[...])
    h = _rms(x1, gfpre_ref[...]).astype(_BF16)
    hbuf[HALO:, :] = h
    acc = jnp.zeros((tm, D_MODEL), _F32)
    off = 0
    for fc in FFN_CHUNKS:
        cs = slice(off, off + fc)
        off += fc
        gbuf[:, 0:fc] = _dot(hbuf[...], wg_ref[:, cs])
        ubuf[:, 0:fc] = _dot(hbuf[...], wu_ref[:, cs])
        g = _causal_taps(gbuf, cwg_ref, slice(0, fc), cs, tm) + cbg_ref[:, cs]
        u = _causal_taps(ubuf, cwu_ref, slice(0, fc), cs, tm) + cbu_ref[:, cs]
        act = (g * jax.nn.sigmoid(g) * u).astype(_BF16)
        acc = acc + _dot(act, wd_ref[cs, :])
    hbuf[0:HALO, :] = h[tm - HALO:, :]
    o_ref[...] = x1 + _rms(acc, gfpost_ref[...])


def _out_ffn(x, att, ycs, p, l, tm):
    b, s, d = x.shape
    tile = lambda w: pl.BlockSpec((None, tm, w), lambda bi, si: (bi, si, 0))

    def layer(a, axis=None, part=0):
        shape, idx = list(a.shape[1:]), [0, 0]
        if axis is not None:
            shape[axis - 1] //= 2
            idx[axis - 1] = part
        return pl.BlockSpec((None, *shape), lambda bi, si: (l, *idx), pipeline_mode=pl.Buffered(1))

    operands = [
        (p["wo"], layer(p["wo"], 1, 0)), (p["wo"], layer(p["wo"], 1, 1)),
        (p["gpost"], layer(p["gpost"])), (p["gfpre"], layer(p["gfpre"])),
        (p["wup"], layer(p["wup"], 2, 0)), (p["wup"], layer(p["wup"], 2, 1)),
        (p["cw"], layer(p["cw"], 2, 0)), (p["cw"], layer(p["cw"], 2, 1)),
        (p["cb"], layer(p["cb"], 2, 0)), (p["cb"], layer(p["cb"], 2, 1)),
        (p["wd"], layer(p["wd"])), (p["gfpost"], layer(p["gfpost"])),
    ]
    weights = [a for a, _ in operands]
    return pl.pallas_call(
        functools.partial(_out_ffn_kernel, tm=tm),
        grid=(b, s // tm),
        in_specs=[tile(d), tile(MLA_OUT), tile(SC_DIM + SSD_DIM)] + [sp for _, sp in operands],
        out_specs=tile(d),
        out_shape=jax.ShapeDtypeStruct((b, s, d), _F32),
        scratch_shapes=[
            pltpu.VMEM((tm + HALO, D_MODEL), _BF16),
            pltpu.VMEM((tm + HALO, max(FFN_CHUNKS)), _F32),
            pltpu.VMEM((tm + HALO, max(FFN_CHUNKS)), _F32),
        ],
        compiler_params=pltpu.CompilerParams(
            dimension_semantics=("arbitrary", "arbitrary"), vmem_limit_bytes=VMEM_LIMIT),
        name="out_ffn",
    )(x, att, ycs, *weights)


def _swap_halves(w):
    half = w.shape[-1] // 2
    return jnp.concatenate([w[..., half:], w[..., :half]], axis=-1)


def _pack_params(norm_mix_pre, norm_mix_post, norm_ffn_pre, norm_ffn_post, w_in, mla_q_norm,
                 mla_w_q_up, mla_kv_norm, mla_w_kv_up, sc_conv_w, ssd_conv_w, ssd_conv_b,
                 ssd_dt_bias, ssd_a_log, ssd_d, ssd_norm, w_out, ffn_w_up, ffn_conv_w, ffn_conv_b,
                 ffn_w_down):
    depth = w_in.shape[0]
    f32 = lambda a: a.astype(_F32)
    row = lambda a: f32(a)[:, None, :]
    splits = np.cumsum(IN_WIDTHS)[:-1]
    w_cq, w_ckv, w_kr, w_scb, w_scc, w_sch, w_ssd = jnp.split(w_in, splits, axis=-1)
    w_z = w_ssd[..., :SSD_DIM]
    w_xbc = w_ssd[..., SSD_DIM:SSD_DIM + SSD_CONV_DIM]
    w_dt = w_ssd[..., SSD_DIM + SSD_CONV_DIM:]
    zeros = jnp.zeros((depth, D_MODEL, MLA_NOPE_DIM - SSD_HEADS), w_in.dtype)
    wmain = jnp.concatenate(
        [w_cq, w_ckv, w_dt, zeros, w_kr, _swap_halves(w_kr), w_scb, w_z], axis=-1)
    wconv = jnp.concatenate([w_scc, w_sch, w_xbc], axis=-1)

    wq4 = mla_w_q_up.reshape(depth, MLA_Q_LORA, MLA_HEADS, MLA_QK_DIM)
    wq = jnp.concatenate([wq4, _swap_halves(wq4[..., MLA_NOPE_DIM:])], axis=-1)
    wq = wq.reshape(depth, MLA_Q_LORA, MLA_HEADS * HEAD_PAD)
    wkv4 = mla_w_kv_up.reshape(depth, MLA_KV_LORA, MLA_HEADS, MLA_NOPE_DIM + MLA_V_DIM)
    wk = jnp.pad(wkv4[..., :MLA_NOPE_DIM], ((0, 0),) * 3 + ((0, HEAD_PAD - MLA_NOPE_DIM),))
    wk = wk.reshape(depth, MLA_KV_LORA, MLA_HEADS * HEAD_PAD)
    wvt = jnp.swapaxes(wkv4[..., MLA_NOPE_DIM:].reshape(depth, MLA_KV_LORA, MLA_OUT), 1, 2)

    pad_heads = lambda a: jnp.pad(f32(a), ((0, 0), (0, LANES - SSD_HEADS)))[:, None, :]
    return {
        "gpre": row(norm_mix_pre), "wmain": wmain.astype(_BF16), "wconv": wconv.astype(_BF16),
        "qn": row(mla_q_norm), "wqt": jnp.swapaxes(wq, 1, 2).astype(_BF16),
        "kvn": row(mla_kv_norm), "wk": wk.astype(_BF16), "wvt": wvt.astype(_BF16),
        "scw": f32(sc_conv_w), "ssw": f32(ssd_conv_w),
        "ssb": row(ssd_conv_b), "dtb": pad_heads(ssd_dt_bias),
        "aneg": -jnp.exp(pad_heads(ssd_a_log)),
        "dskip": jnp.repeat(f32(ssd_d), SSD_HEAD_DIM, axis=-1)[:, None, :],
        "snorm": row(ssd_norm),
        "wo": w_out.astype(_BF16), "gpost": row(norm_mix_post), "gfpre": row(norm_ffn_pre),
        "wup": ffn_w_up.astype(_BF16), "cw": f32(ffn_conv_w), "cb": row(ffn_conv_b),
        "wd": ffn_w_down.astype(_BF16), "gfpost": row(norm_ffn_post),
    }


def _rope_tables(positions):
    inv_freq = 1.0 / (ROPE_THETA ** (jnp.arange(0, MLA_ROPE_DIM, 2, dtype=_F32) / MLA_ROPE_DIM))
    ang = positions.astype(_F32)[..., None] * inv_freq
    cos = jnp.cos(ang)
    sin = jnp.sin(ang)
    lead = positions.shape + (MLA_NOPE_DIM,)
    tail = positions.shape + (LANES - MLA_NOPE_DIM - MLA_ROPE_DIM,)
    c1 = jnp.concatenate([jnp.ones(lead, _F32), cos, cos, jnp.zeros(tail, _F32)], axis=-1)
    c2 = jnp.concatenate([jnp.zeros(lead, _F32), -sin, sin, jnp.zeros(tail, _F32)], axis=-1)
    return c1, c2, jnp.swapaxes(c1, 1, 2), jnp.swapaxes(c2, 1, 2)


def kernel(x, positions, norm_mix_pre, norm_mix_post, norm_ffn_pre, norm_ffn_post, w_in, mla_q_norm, mla_w_q_up, mla_kv_norm, mla_w_kv_up, sc_conv_w, ssd_conv_w, ssd_conv_b, ssd_dt_bias, ssd_a_log, ssd_d, ssd_norm, w_out, ffn_w_up, ffn_conv_w, ffn_conv_b, ffn_w_down):
    depth = w_in.shape[0]
    s = x.shape[1]
    ts = min(SEQ_TILE_MIX, s)
    ta = min(SEQ_TILE_ATT, s)
    tm = min(SEQ_TILE_FFN, s)
    assert s % ts == 0 and s % ta == 0 and s % tm == 0 and ts % SSD_CHUNK == 0
    rope = _rope_tables(positions)
    p = _pack_params(norm_mix_pre, norm_mix_post, norm_ffn_pre, norm_ffn_post, w_in, mla_q_norm,
                     mla_w_q_up, mla_kv_norm, mla_w_kv_up, sc_conv_w, ssd_conv_w, ssd_conv_b,
                     ssd_dt_bias, ssd_a_log, ssd_d, ssd_norm, w_out, ffn_w_up, ffn_conv_w,
                     ffn_conv_b, ffn_w_down)
    for l in range(depth):
        qt, k, vt, ycs = _mixer_in(x, rope, p, l, ts)
        att = _attention(qt, k, vt, ta)
        x = _out_ffn(x, att, ycs, p, l, tm)
    return x
```

```python
import functools
import math

import jax
import jax.numpy as jnp
import numpy as np
from jax import lax
from jax.experimental import pallas as pl
from jax.experimental.pallas import tpu as pltpu

D_MODEL = 1024
MLA_HEADS = 8
MLA_Q_LORA = 256
MLA_KV_LORA = 128
MLA_NOPE_DIM = 64
MLA_ROPE_DIM = 32
MLA_V_DIM = 64
ROPE_THETA = 10000.0
SC_DIM = 256
SC_WIDTH = 3
SSD_HEADS = 4
SSD_HEAD_DIM = 64
SSD_GROUPS = 2
SSD_STATE = 128
SSD_CONV_WIDTH = 4
SSD_CHUNK = 128
FFN_DIM = 2816
FFN_CONV_WIDTH = 3
NORM_EPS = 1e-6

MLA_QK_DIM = MLA_NOPE_DIM + MLA_ROPE_DIM
MLA_OUT = MLA_HEADS * MLA_V_DIM
SSD_DIM = SSD_HEADS * SSD_HEAD_DIM
SSD_BC_DIM = SSD_GROUPS * SSD_STATE
SSD_CONV_DIM = SSD_DIM + 2 * SSD_BC_DIM
SSD_IN = SSD_DIM + SSD_CONV_DIM + SSD_HEADS
IN_WIDTHS = (MLA_Q_LORA, MLA_KV_LORA, MLA_ROPE_DIM, SC_DIM, SC_DIM, SC_DIM, SSD_IN)
D_IN = sum(IN_WIDTHS)

LANES = 128
MXU_COLS = 256
HEAD_PAD = LANES
HALO = 16

_M_CQ = 0
_M_CKV = _M_CQ + MLA_Q_LORA
_M_KRDT = _M_CKV + MLA_KV_LORA
_M_SCB = _M_KRDT + LANES
_M_Z = _M_SCB + SC_DIM
_M_END = _M_Z + SSD_DIM
_C_SCC = 0
_C_SCH = _C_SCC + SC_DIM
_C_XBC = _C_SCH + SC_DIM
_C_END = _C_XBC + SSD_CONV_DIM

SEQ_TILE_MIX = 512
SEQ_TILE_ATT = 512
ATT_SUM_ROWS = 16
SEQ_TILE_FFN = 512
FFN_CHUNKS = (FFN_DIM,)
VMEM_LIMIT = 56 * 1024 * 1024

assert _M_END % MXU_COLS == 0 and _C_END % MXU_COLS == 0
assert SSD_HEADS == 2 * SSD_GROUPS and 2 * SSD_HEAD_DIM == LANES
assert sum(FFN_CHUNKS) == FFN_DIM and all(c % MXU_COLS == 0 for c in FFN_CHUNKS)

_F32 = jnp.float32
_BF16 = jnp.bfloat16


def _rms(x, w):
    return x * lax.rsqrt(jnp.mean(x * x, axis=-1, keepdims=True) + NORM_EPS) * w


def _dot(a, b):
    return jnp.dot(a, b, preferred_element_type=_F32)


def _dot_nt(a, b):
    return lax.dot_general(a, b, (((1,), (1,)), ((), ())), preferred_element_type=_F32)


def _dot_tn(a, b):
    return lax.dot_general(a, b, (((0,), (0,)), ((), ())), preferred_element_type=_F32)


def _dot_f32(a, b):
    return jnp.dot(a, b, preferred_element_type=_F32, precision=lax.Precision.HIGHEST)


def _causal_taps(buf, w_ref, cols, wcols, rows):
    width = w_ref.shape[0]
    out = buf[HALO:HALO + rows, cols] * w_ref[width - 1:width, wcols]
    for i in range(width - 1):
        o = HALO - width + 1 + i
        out = out + buf[o:o + rows, cols] * w_ref[i:i + 1, wcols]
    return out


def _rope_block(a, c1, c2):
    return a * c1 + pltpu.roll(a, LANES - MLA_ROPE_DIM, 1) * c2


def _mixer_project(x_ref, gpre_ref, wmain_ref, wconv_ref, hbuf, pm_ref, pc_ref, ts):
    hbuf[HALO:, :] = _rms(x_ref[...], gpre_ref[...]).astype(_BF16)
    pieces = []
    for c0 in range(0, _M_END, MXU_COLS):
        def main_piece(cs=slice(c0, c0 + MXU_COLS)):
            pm_ref[:, cs] = _dot(hbuf[HALO:, :], wmain_ref[:, cs])
        pieces.append(main_piece)
    for c0 in range(0, _C_END, MXU_COLS):
        def conv_piece(cs=slice(c0, c0 + MXU_COLS)):
            pc_ref[:, cs] = _dot(hbuf[...], wconv_ref[:, cs])
        pieces.append(conv_piece)

    def keep_halo():
        hbuf[0:HALO, :] = hbuf[ts:ts + HALO, :]
    pieces.append(keep_halo)
    return pieces


def _mixer_heads(pm, pcbuf, c1_ref, c2_ref, c1t_ref, c2t_ref, qn_ref, wqt_ref, kvn_ref, wk_ref,
                 wvt_ref, scw_ref, ssw_ref, ssb_ref, dtb_ref, aneg_ref, dskip_ref, snorm_ref,
                 qt_ref, k_ref, vt_ref, ycs_ref, ubuf, ybuf, state_ref, ts, pieces):
    def tick():
        if pieces:
            pieces.pop(0)()

    scale = MLA_QK_DIM ** -0.5 * math.log2(math.e)
    lane = lax.broadcasted_iota(jnp.int32, (1, LANES), 1)

    qn = _rms(pm[:, _M_CQ:_M_CQ + MLA_Q_LORA], qn_ref[...]).astype(_BF16)
    qt = _dot_nt(wqt_ref[...], qn)
    c1t = c1t_ref[...]
    c2t = c2t_ref[...]
    for hd in range(MLA_HEADS):
        a = qt[hd * HEAD_PAD:(hd + 1) * HEAD_PAD, :]
        a_swapped = jnp.concatenate([a[MLA_ROPE_DIM:, :], a[:MLA_ROPE_DIM, :]], axis=0)
        qt_ref[hd * HEAD_PAD:(hd + 1) * HEAD_PAD, :] = (
            (a * c1t + a_swapped * c2t) * scale).astype(_BF16)
        if hd == MLA_HEADS - 1:
            tick()
    kvn = _rms(pm[:, _M_CKV:_M_CKV + MLA_KV_LORA], kvn_ref[...]).astype(_BF16)
    vt_ref[...] = _dot_nt(wvt_ref[...], kvn).astype(_BF16)
    kfull = _dot(kvn, wk_ref[...])
    krdt = pm[:, _M_KRDT:_M_KRDT + LANES]
    kr = _rope_block(krdt, jnp.where(lane < MLA_NOPE_DIM, 0.0, c1_ref[...]), c2_ref[...])
    for hd in range(MLA_HEADS):
        sl = slice(hd * HEAD_PAD, (hd + 1) * HEAD_PAD)
        k_ref[:, sl] = (kfull[:, sl] + kr).astype(_BF16)

    ubuf[...] = pcbuf[:, _C_SCC:_C_SCC + SC_DIM] * pcbuf[:, _C_SCH:_C_SCH + SC_DIM]
    conv = _causal_taps(ubuf, scw_ref, slice(0, SC_DIM), slice(0, SC_DIM), ts)
    ycs_ref[:, 0:SC_DIM] = (pm[:, _M_SCB:_M_SCB + SC_DIM] * conv).astype(_BF16)

    xbc = _causal_taps(pcbuf, ssw_ref, slice(_C_XBC, _C_END), slice(0, SSD_CONV_DIM), ts)
    xbc = xbc + ssb_ref[...]
    xbc = xbc * jax.nn.sigmoid(xbc)
    z = pm[:, _M_Z:_M_Z + SSD_DIM]
    dt_raw = krdt + dtb_ref[...]
    dt = jnp.maximum(dt_raw, 0.0) + jnp.log1p(jnp.exp(-jnp.abs(dt_raw)))
    adt = dt * aneg_ref[...]
    tick()

    L = SSD_CHUNK
    nchunk = ts // L
    rows = lax.broadcasted_iota(jnp.int32, (L, L), 0)
    cols = lax.broadcasted_iota(jnp.int32, (L, L), 1)
    tri = rows >= cols
    head_lanes = lane < SSD_HEADS
    packed = jnp.where(head_lanes, adt[0:L, :], 0.0)
    for c in range(1, nchunk):
        packed = packed + pltpu.roll(jnp.where(head_lanes, adt[c * L:(c + 1) * L, :], 0.0),
                                     c * SSD_HEADS, 1)
    a_cs = _dot_f32(tri.astype(_F32), packed)
    a_cs_t = a_cs.T
    n = SSD_STATE
    first = lane < SSD_HEAD_DIM
    row2 = lax.broadcasted_iota(jnp.int32, (2 * n, LANES), 0)
    lane2 = lax.broadcasted_iota(jnp.int32, (2 * n, LANES), 1)
    diag_blocks = (row2 < n) == (lane2 < SSD_HEAD_DIM)
    states = [state_ref[g] for g in range(SSD_GROUPS)]
    for c in range(nchunk):
        r = slice(c * L, (c + 1) * L)
        dt_c = dt[r, :]
        for g in range(SSD_GROUPS):
            gsl = slice(g * LANES, (g + 1) * LANES)
            xs_g = xbc[r, gsl]
            bg = xbc[r, SSD_DIM + g * n:SSD_DIM + (g + 1) * n]
            cg = xbc[r, SSD_DIM + SSD_BC_DIM + g * n:SSD_DIM + SSD_BC_DIM + (g + 1) * n]
            sc = _dot_nt(cg.astype(_BF16), bg.astype(_BF16))
            tick()
            h0 = g * 2
            ln = c * SSD_HEADS + h0
            col = [a_cs[:, ln + k:ln + k + 1] for k in range(2)]
            row = [a_cs_t[ln + k:ln + k + 1, :] for k in range(2)]
            end = [a_cs[L - 1:L, ln + k:ln + k + 1] for k in range(2)]
            xdt = xs_g * jnp.where(first, dt_c[:, h0:h0 + 1], dt_c[:, h0 + 1:h0 + 2])
            xdt_blocks = jnp.concatenate(
                [jnp.where(first, xdt, 0.0).astype(_BF16), jnp.where(first, 0.0, xdt).astype(_BF16)],
                axis=0)
            dec = [jnp.exp(jnp.where(tri, col[k] - row[k], -jnp.inf)) for k in range(2)]
            m_g = jnp.concatenate([(sc * dec[k]).astype(_BF16) for k in range(2)], axis=1)
            y = _dot(m_g, xdt_blocks)
            cge = jnp.concatenate([(cg * jnp.exp(col[k])).astype(_BF16) for k in range(2)], axis=1)
            y = y + _dot(cge, states[g].astype(_BF16))
            bd = jnp.concatenate(
                [(bg * jnp.exp(end[k] - col[k])).astype(_BF16) for k in range(2)], axis=1)
            upd = _dot_tn(bd, xdt.astype(_BF16))
            decay = jnp.where(row2 < n, jnp.exp(end[0]), jnp.exp(end[1]))
            states[g] = states[g] * decay + jnp.where(diag_blocks, upd, 0.0)
            ybuf[r, gsl] = y + xs_g * dskip_ref[:, gsl]
    for g in range(SSD_GROUPS):
        state_ref[g] = states[g]
    while pieces:
        tick()
    yg = ybuf[...] * (z * jax.nn.sigmoid(z))
    ycs_ref[:, SC_DIM:SC_DIM + SSD_DIM] = _rms(yg, snorm_ref[...]).astype(_BF16)


def _mixer_in_kernel(x_ref, c1_ref, c2_ref, c1t_ref, c2t_ref, gpre_ref, wmain_ref, wconv_ref,
                     qn_ref, wqt_ref, kvn_ref, wk_ref, wvt_ref, scw_ref, ssw_ref, ssb_ref, dtb_ref,
                     aneg_ref, dskip_ref, snorm_ref,
                     qt_ref, k_ref, vt_ref, ycs_ref,
                     hbuf, pm0, pm1, pc0, pc1, ubuf, ybuf, state_ref, *, ts, nt):
    i = pl.program_id(0)

    @pl.when(i == 0)
    def _():
        pm1[...] = jnp.zeros_like(pm1)
        pc1[...] = jnp.zeros_like(pc1)
        state_ref[...] = jnp.zeros_like(state_ref)

    @pl.when(lax.rem(i, nt) == 0)
    def _():
        hbuf[0:HALO, :] = jnp.zeros((HALO, D_MODEL), _BF16)

    @pl.when(lax.rem(i + nt - 1, nt) == 0)
    def _():
        state_ref[...] = jnp.zeros_like(state_ref)

    def step(pm_w, pc_w, pm_r, pc_r):
        pieces = _mixer_project(x_ref, gpre_ref, wmain_ref, wconv_ref, hbuf, pm_w, pc_w, ts)
        _mixer_heads(pm_r, pc_r, c1_ref, c2_ref, c1t_ref, c2t_ref, qn_ref, wqt_ref, kvn_ref,
                     wk_ref, wvt_ref, scw_ref, ssw_ref, ssb_ref, dtb_ref, aneg_ref, dskip_ref,
                     snorm_ref, qt_ref, k_ref, vt_ref, ycs_ref, ubuf, ybuf, state_ref, ts, pieces)

    @pl.when(lax.rem(i, 2) == 0)
    def _():
        step(pm0, pc0, pm1, pc1)

    @pl.when(lax.rem(i, 2) == 1)
    def _():
        step(pm1, pc1, pm0, pc0)


def _mixer_in(x, rope, p, l, ts):
    b, s, d = x.shape
    nt = s // ts
    ntiles = b * nt
    t_in = lambda i: jnp.minimum(i, ntiles - 1)
    t_out = lambda i: jnp.maximum(i - 1, 0)
    tile = lambda w, t: pl.BlockSpec((None, ts, w), lambda i: (t(i) // nt, t(i) % nt, 0))
    tile_t = lambda w, t: pl.BlockSpec((None, w, ts), lambda i: (t(i) // nt, 0, t(i) % nt))
    layer = lambda a: pl.BlockSpec((None,) + a.shape[1:], lambda i: (l, 0, 0))
    c1, c2, c1t, c2t = rope
    weights = [p["gpre"], p["wmain"], p["wconv"], p["qn"], p["wqt"], p["kvn"], p["wk"], p["wvt"],
               p["scw"], p["ssw"], p["ssb"], p["dtb"], p["aneg"], p["dskip"], p["snorm"]]
    out_shape = [
        jax.ShapeDtypeStruct((b, MLA_HEADS * HEAD_PAD, s), _BF16),
        jax.ShapeDtypeStruct((b, s, MLA_HEADS * HEAD_PAD), _BF16),
        jax.ShapeDtypeStruct((b, MLA_OUT, s), _BF16),
        jax.ShapeDtypeStruct((b, s, SC_DIM + SSD_DIM), _BF16),
    ]
    return pl.pallas_call(
        functools.partial(_mixer_in_kernel, ts=ts, nt=nt),
        grid=(ntiles + 1,),
        in_specs=[tile(d, t_in), tile(LANES, t_out), tile(LANES, t_out),
                  tile_t(LANES, t_out), tile_t(LANES, t_out)]
        + [layer(w) for w in weights],
        out_specs=[tile_t(MLA_HEADS * HEAD_PAD, t_out), tile(MLA_HEADS * HEAD_PAD, t_out),
                   tile_t(MLA_OUT, t_out), tile(SC_DIM + SSD_DIM, t_out)],
        out_shape=out_shape,
        scratch_shapes=[
            pltpu.VMEM((ts + HALO, D_MODEL), _BF16),
            pltpu.VMEM((ts, _M_END), _F32), pltpu.VMEM((ts, _M_END), _F32),
            pltpu.VMEM((ts + HALO, _C_END), _F32), pltpu.VMEM((ts + HALO, _C_END), _F32),
            pltpu.VMEM((ts + HALO, SC_DIM), _F32),
            pltpu.VMEM((ts, SSD_DIM), _F32),
            pltpu.VMEM((SSD_GROUPS, 2 * SSD_STATE, LANES), _F32),
        ],
        compiler_params=pltpu.CompilerParams(
            dimension_semantics=("arbitrary",), vmem_limit_bytes=VMEM_LIMIT),
        name="mixer_in",
    )(x, c1, c2, c1t, c2t, *weights)


def _attention_kernel(qt_ref, k_ref, vt_ref, o_ref, s0_ref, s1_ref, m_ref, acc_ref, *, t):
    qi = pl.program_id(1)
    half = t // 2
    tri = (lax.broadcasted_iota(jnp.int32, (half, half), 0)
           <= lax.broadcasted_iota(jnp.int32, (half, half), 1))
    sbufs = (s0_ref, s1_ref)
    hsl = lambda hd: slice(hd * HEAD_PAD, (hd + 1) * HEAD_PAD)
    vsl = lambda hd: slice(hd * MLA_V_DIM, (hd + 1) * MLA_V_DIM)

    def online(st, vt, m, acc):
        m_new = jnp.maximum(m, jnp.max(st, axis=0, keepdims=True))
        alpha = jnp.exp2(m - m_new)
        pt = jnp.exp2(st - m_new).astype(_BF16)
        lhs = jnp.concatenate([vt, jnp.ones((ATT_SUM_ROWS, vt.shape[1]), _BF16)], axis=0)
        return m_new, alpha * acc + _dot(lhs, pt)

    def scores(hd, j):
        start = pl.multiple_of(j * t, t)
        sbufs[hd % 2][...] = _dot(k_ref[pl.ds(start, t), hsl(hd)], qt_ref[hsl(hd), :])

    def softmax_pv(hd, j):
        start = pl.multiple_of(j * t, t)
        m, acc = online(sbufs[hd % 2][...], vt_ref[vsl(hd), pl.ds(start, t)],
                        m_ref[hd], acc_ref[hd])
        m_ref[hd] = m
        acc_ref[hd] = acc

    def scores_diag(hd):
        lo = pl.multiple_of(qi * t, t)
        hi = pl.multiple_of(qi * t + half, half)
        sref = sbufs[hd % 2]
        sref[0:half, :] = _dot(k_ref[pl.ds(lo, half), hsl(hd)], qt_ref[hsl(hd), :])
        sref[half:, half:] = _dot(k_ref[pl.ds(hi, half), hsl(hd)], qt_ref[hsl(hd), half:])

    def softmax_pv_diag(hd):
        lo = pl.multiple_of(qi * t, t)
        hi = pl.multiple_of(qi * t + half, half)
        sref = sbufs[hd % 2]
        st = jnp.concatenate([jnp.where(tri, sref[0:half, 0:half], -jnp.inf),
                              sref[0:half, half:]], axis=1)
        m, acc = online(st, vt_ref[vsl(hd), pl.ds(lo, half)], m_ref[hd], acc_ref[hd])
        st = jnp.where(tri, sref[half:, half:], -jnp.inf)
        _, acc_hi = online(st, vt_ref[vsl(hd), pl.ds(hi, half)], m[:, half:], acc[:, half:])
        acc = jnp.concatenate([acc[:, :half], acc_hi], axis=1)
        o = acc[0:MLA_V_DIM, :] / acc[MLA_V_DIM:MLA_V_DIM + 1, :]
        o_ref[:, vsl(hd)] = o.T.astype(_BF16)

    m_ref[...] = jnp.full(m_ref.shape, -jnp.inf, _F32)
    acc_ref[...] = jnp.zeros(acc_ref.shape, _F32)

    scores(0, 0)

    def chunk_items(j):
        for hd in range(MLA_HEADS):
            if hd + 1 < MLA_HEADS:
                scores(hd + 1, j)
            else:
                scores(0, j + 1)
            softmax_pv(hd, j)

    def body(jj, carry):
        chunk_items(2 * jj)
        chunk_items(2 * jj + 1)
        return carry

    lax.fori_loop(0, lax.shift_right_logical(qi, 1), body, 0)

    @pl.when(lax.rem(qi, 2) == 1)
    def _():
        chunk_items(qi - 1)

    for hd in range(MLA_HEADS):
        if hd + 1 < MLA_HEADS:
            scores_diag(hd + 1)
        softmax_pv_diag(hd)


def _attention(qt, k, vt, t):
    b, s, _ = k.shape
    return pl.pallas_call(
        functools.partial(_attention_kernel, t=t),
        grid=(b, s // t),
        in_specs=[
            pl.BlockSpec((None, MLA_HEADS * HEAD_PAD, t), lambda bi, qi: (bi, 0, qi)),
            pl.BlockSpec((None, s, MLA_HEADS * HEAD_PAD), lambda bi, qi: (bi, 0, 0)),
            pl.BlockSpec((None, MLA_OUT, s), lambda bi, qi: (bi, 0, 0)),
        ],
        out_specs=pl.BlockSpec((None, t, MLA_OUT), lambda bi, qi: (bi, qi, 0)),
        out_shape=jax.ShapeDtypeStruct((b, s, MLA_OUT), _BF16),
        scratch_shapes=[
            pltpu.VMEM((t, t), _F32), pltpu.VMEM((t, t), _F32),
            pltpu.VMEM((MLA_HEADS, 1, t), _F32),
            pltpu.VMEM((MLA_HEADS, MLA_V_DIM + ATT_SUM_ROWS, t), _F32),
        ],
        compiler_params=pltpu.CompilerParams(
            dimension_semantics=("arbitrary", "arbitrary"), vmem_limit_bytes=VMEM_LIMIT),
        name="attention",
    )(qt, k, vt)


def _out_ffn_kernel(x_ref, att_ref, ycs_ref, woa_ref, woc_ref, gpost_ref, gfpre_ref, wg_ref,
                    wu_ref, cwg_ref, cwu_ref, cbg_ref, cbu_ref, wd_ref, gfpost_ref,
                    o_ref, hbuf, gbuf, ubuf, *, tm):
    si = pl.program_id(1)

    @pl.when(si == 0)
    def _():
        hbuf[0:HALO, :] = jnp.zeros((HALO, D_MODEL), _BF16)

    mixed = _dot(att_ref[...], woa_ref[...]) + _dot(ycs_ref[...], woc_ref[...])
    x1 = x_ref[...] + _rms(mixed, gpost_ref[...])
    h = _rms(x1, gfpre_ref[...]).astype(_BF16)
    hbuf[HALO:, :] = h
    acc = jnp.zeros((tm, D_MODEL), _F32)
    off = 0
    for fc in FFN_CHUNKS:
        cs = slice(off, off + fc)
        off += fc
        gbuf[:, 0:fc] = _dot(hbuf[...], wg_ref[:, cs])
        ubuf[:, 0:fc] = _dot(hbuf[...], wu_ref[:, cs])
        g = _causal_taps(gbuf, cwg_ref, slice(0, fc), cs, tm) + cbg_ref[:, cs]
        u = _causal_taps(ubuf, cwu_ref, slice(0, fc), cs, tm) + cbu_ref[:, cs]
        act = (g * jax.nn.sigmoid(g) * u).astype(_BF16)
        acc = acc + _dot(act, wd_ref[cs, :])
    hbuf[0:HALO, :] = h[tm - HALO:, :]
    o_ref[...] = x1 + _rms(acc, gfpost_ref[...])


def _out_ffn(x, att, ycs, p, l, tm):
    b, s, d = x.shape
    tile = lambda w: pl.BlockSpec((None, tm, w), lambda bi, si: (bi, si, 0))

    def layer(a, axis=None, part=0):
        shape, idx = list(a.shape[1:]), [0, 0]
        if axis is not None:
            shape[axis - 1] //= 2
            idx[axis - 1] = part
        return pl.BlockSpec((None, *shape), lambda bi, si: (l, *idx), pipeline_mode=pl.Buffered(1))

    operands = [
        (p["wo"], layer(p["wo"], 1, 0)), (p["wo"], layer(p["wo"], 1, 1)),
        (p["gpost"], layer(p["gpost"])), (p["gfpre"], layer(p["gfpre"])),
        (p["wup"], layer(p["wup"], 2, 0)), (p["wup"], layer(p["wup"], 2, 1)),
        (p["cw"], layer(p["cw"], 2, 0)), (p["cw"], layer(p["cw"], 2, 1)),
        (p["cb"], layer(p["cb"], 2, 0)), (p["cb"], layer(p["cb"], 2, 1)),
        (p["wd"], layer(p["wd"])), (p["gfpost"], layer(p["gfpost"])),
    ]
    weights = [a for a, _ in operands]
    return pl.pallas_call(
        functools.partial(_out_ffn_kernel, tm=tm),
        grid=(b, s // tm),
        in_specs=[tile(d), tile(MLA_OUT), tile(SC_DIM + SSD_DIM)] + [sp for _, sp in operands],
        out_specs=tile(d),
        out_shape=jax.ShapeDtypeStruct((b, s, d), _F32),
        scratch_shapes=[
            pltpu.VMEM((tm + HALO, D_MODEL), _BF16),
            pltpu.VMEM((tm + HALO, max(FFN_CHUNKS)), _F32),
            pltpu.VMEM((tm + HALO, max(FFN_CHUNKS)), _F32),
        ],
        compiler_params=pltpu.CompilerParams(
            dimension_semantics=("arbitrary", "arbitrary"), vmem_limit_bytes=VMEM_LIMIT),
        name="out_ffn",
    )(x, att, ycs, *weights)


def _swap_halves(w):
    half = w.shape[-1] // 2
    return jnp.concatenate([w[..., half:], w[..., :half]], axis=-1)


def _pack_params(norm_mix_pre, norm_mix_post, norm_ffn_pre, norm_ffn_post, w_in, mla_q_norm,
                 mla_w_q_up, mla_kv_norm, mla_w_kv_up, sc_conv_w, ssd_conv_w, ssd_conv_b,
                 ssd_dt_bias, ssd_a_log, ssd_d, ssd_norm, w_out, ffn_w_up, ffn_conv_w, ffn_conv_b,
                 ffn_w_down):
    depth = w_in.shape[0]
    f32 = lambda a: a.astype(_F32)
    row = lambda a: f32(a)[:, None, :]
    splits = np.cumsum(IN_WIDTHS)[:-1]
    w_cq, w_ckv, w_kr, w_scb, w_scc, w_sch, w_ssd = jnp.split(w_in, splits, axis=-1)
    w_z = w_ssd[..., :SSD_DIM]
    w_xbc = w_ssd[..., SSD_DIM:SSD_DIM + SSD_CONV_DIM]
    w_dt = w_ssd[..., SSD_DIM + SSD_CONV_DIM:]
    zeros = jnp.zeros((depth, D_MODEL, MLA_NOPE_DIM - SSD_HEADS), w_in.dtype)
    wmain = jnp.concatenate(
        [w_cq, w_ckv, w_dt, zeros, w_kr, _swap_halves(w_kr), w_scb, w_z], axis=-1)
    wconv = jnp.concatenate([w_scc, w_sch, w_xbc], axis=-1)

    wq4 = mla_w_q_up.reshape(depth, MLA_Q_LORA, MLA_HEADS, MLA_QK_DIM)
    wq = jnp.concatenate([wq4, _swap_halves(wq4[..., MLA_NOPE_DIM:])], axis=-1)
    wq = wq.reshape(depth, MLA_Q_LORA, MLA_HEADS * HEAD_PAD)
    wkv4 = mla_w_kv_up.reshape(depth, MLA_KV_LORA, MLA_HEADS, MLA_NOPE_DIM + MLA_V_DIM)
    wk = jnp.pad(wkv4[..., :MLA_NOPE_DIM], ((0, 0),) * 3 + ((0, HEAD_PAD - MLA_NOPE_DIM),))
    wk = wk.reshape(depth, MLA_KV_LORA, MLA_HEADS * HEAD_PAD)
    wvt = jnp.swapaxes(wkv4[..., MLA_NOPE_DIM:].reshape(depth, MLA_KV_LORA, MLA_OUT), 1, 2)

    pad_heads = lambda a: jnp.pad(f32(a), ((0, 0), (0, LANES - SSD_HEADS)))[:, None, :]
    return {
        "gpre": row(norm_mix_pre), "wmain": wmain.astype(_BF16), "wconv": wconv.astype(_BF16),
        "qn": row(mla_q_norm), "wqt": jnp.swapaxes(wq, 1, 2).astype(_BF16),
        "kvn": row(mla_kv_norm), "wk": wk.astype(_BF16), "wvt": wvt.astype(_BF16),
        "scw": f32(sc_conv_w), "ssw": f32(ssd_conv_w),
        "ssb": row(ssd_conv_b), "dtb": pad_heads(ssd_dt_bias),
        "aneg": -jnp.exp(pad_heads(ssd_a_log)),
        "dskip": jnp.repeat(f32(ssd_d), SSD_HEAD_DIM, axis=-1)[:, None, :],
        "snorm": row(ssd_norm),
        "wo": w_out.astype(_BF16), "gpost": row(norm_mix_post), "gfpre": row(norm_ffn_pre),
        "wup": ffn_w_up.astype(_BF16), "cw": f32(ffn_conv_w), "cb": row(ffn_conv_b),
        "wd": ffn_w_down.astype(_BF16), "gfpost": row(norm_ffn_post),
    }


def _rope_tables(positions):
    inv_freq = 1.0 / (ROPE_THETA ** (jnp.arange(0, MLA_ROPE_DIM, 2, dtype=_F32) / MLA_ROPE_DIM))
    ang = positions.astype(_F32)[..., None] * inv_freq
    cos = jnp.cos(ang)
    sin = jnp.sin(ang)
    lead = positions.shape + (MLA_NOPE_DIM,)
    tail = positions.shape + (LANES - MLA_NOPE_DIM - MLA_ROPE_DIM,)
    c1 = jnp.concatenate([jnp.ones(lead, _F32), cos, cos, jnp.zeros(tail, _F32)], axis=-1)
    c2 = jnp.concatenate([jnp.zeros(lead, _F32), -sin, sin, jnp.zeros(tail, _F32)], axis=-1)
    return c1, c2, jnp.swapaxes(c1, 1, 2), jnp.swapaxes(c2, 1, 2)


def kernel(x, positions, norm_mix_pre, norm_mix_post, norm_ffn_pre, norm_ffn_post, w_in, mla_q_norm, mla_w_q_up, mla_kv_norm, mla_w_kv_up, sc_conv_w, ssd_conv_w, ssd_conv_b, ssd_dt_bias, ssd_a_log, ssd_d, ssd_norm, w_out, ffn_w_up, ffn_conv_w, ffn_conv_b, ffn_w_down):
    depth = w_in.shape[0]
    s = x.shape[1]
    ts = min(SEQ_TILE_MIX, s)
    ta = min(SEQ_TILE_ATT, s)
    tm = min(SEQ_TILE_FFN, s)
    assert s % ts == 0 and s % ta == 0 and s % tm == 0 and ts % SSD_CHUNK == 0
    rope = _rope_tables(positions)
    p = _pack_params(norm_mix_pre, norm_mix_post, norm_ffn_pre, norm_ffn_post, w_in, mla_q_norm,
                     mla_w_q_up, mla_kv_norm, mla_w_kv_up, sc_conv_w, ssd_conv_w, ssd_conv_b,
                     ssd_dt_bias, ssd_a_log, ssd_d, ssd_norm, w_out, ffn_w_up, ffn_conv_w,
                     ffn_conv_b, ffn_w_down)
    for l in range(depth):
        qt, k, vt, ycs = _mixer_in(x, rope, p, l, ts)
        att = _attention(qt, k, vt, ta)
        x = _out_ffn(x, att, ycs, p, l, tm)
    return x
```

```python
import functools
import math

import jax
import jax.numpy as jnp
import numpy as np
from jax import lax
from jax.experimental import pallas as pl
from jax.experimental.pallas import tpu as pltpu

D_MODEL = 1024
MLA_HEADS = 8
MLA_Q_LORA = 256
MLA_KV_LORA = 128
MLA_NOPE_DIM = 64
MLA_ROPE_DIM = 32
MLA_V_DIM = 64
ROPE_THETA = 10000.0
SC_DIM = 256
SC_WIDTH = 3
SSD_HEADS = 4
SSD_HEAD_DIM = 64
SSD_GROUPS = 2
SSD_STATE = 128
SSD_CONV_WIDTH = 4
SSD_CHUNK = 128
FFN_DIM = 2816
FFN_CONV_WIDTH = 3
NORM_EPS = 1e-6

MLA_QK_DIM = MLA_NOPE_DIM + MLA_ROPE_DIM
MLA_OUT = MLA_HEADS * MLA_V_DIM
SSD_DIM = SSD_HEADS * SSD_HEAD_DIM
SSD_BC_DIM = SSD_GROUPS * SSD_STATE
SSD_CONV_DIM = SSD_DIM + 2 * SSD_BC_DIM
SSD_IN = SSD_DIM + SSD_CONV_DIM + SSD_HEADS
IN_WIDTHS = (MLA_Q_LORA, MLA_KV_LORA, MLA_ROPE_DIM, SC_DIM, SC_DIM, SC_DIM, SSD_IN)
D_IN = sum(IN_WIDTHS)

LANES = 128
MXU_COLS = 256
HEAD_PAD = LANES
HALO = 16

_M_CQ = 0
_M_CKV = _M_CQ + MLA_Q_LORA
_M_KRDT = _M_CKV + MLA_KV_LORA
_M_SCB = _M_KRDT + LANES
_M_Z = _M_SCB + SC_DIM
_M_END = _M_Z + SSD_DIM
_C_SCC = 0
_C_SCH = _C_SCC + SC_DIM
_C_XBC = _C_SCH + SC_DIM
_C_END = _C_XBC + SSD_CONV_DIM

SEQ_TILE_MIX = 512
SEQ_TILE_ATT = 512
ATT_SUM_ROWS = 16
SEQ_TILE_FFN = 512
VMEM_LIMIT = 56 * 1024 * 1024

assert _M_END % MXU_COLS == 0 and _C_END % MXU_COLS == 0
assert SSD_HEADS == 2 * SSD_GROUPS and 2 * SSD_HEAD_DIM == LANES
assert FFN_DIM % MXU_COLS == 0

_F32 = jnp.float32
_BF16 = jnp.bfloat16


def _rms(x, w):
    return x * lax.rsqrt(jnp.mean(x * x, axis=-1, keepdims=True) + NORM_EPS) * w


def _dot(a, b):
    return jnp.dot(a, b, preferred_element_type=_F32)


def _dot_nt(a, b):
    return lax.dot_general(a, b, (((1,), (1,)), ((), ())), preferred_element_type=_F32)


def _dot_tn(a, b):
    return lax.dot_general(a, b, (((0,), (0,)), ((), ())), preferred_element_type=_F32)


def _dot_f32(a, b):
    return jnp.dot(a, b, preferred_element_type=_F32, precision=lax.Precision.HIGHEST)


def _causal_taps(buf, w_ref, cols, wcols, rows):
    width = w_ref.shape[0]
    out = buf[HALO:HALO + rows, cols] * w_ref[width - 1:width, wcols]
    for i in range(width - 1):
        o = HALO - width + 1 + i
        out = out + buf[o:o + rows, cols] * w_ref[i:i + 1, wcols]
    return out


def _rope_block(a, c1, c2):
    return a * c1 + pltpu.roll(a, LANES - MLA_ROPE_DIM, 1) * c2


def _mixer_project(x_ref, gpre_ref, wmain_ref, wconv_ref, hbuf, pm_ref, pc_ref, ts):
    hbuf[HALO:, :] = _rms(x_ref[...], gpre_ref[...]).astype(_BF16)
    pieces = []
    for c0 in range(0, _M_END, MXU_COLS):
        def main_piece(cs=slice(c0, c0 + MXU_COLS)):
            pm_ref[:, cs] = _dot(hbuf[HALO:, :], wmain_ref[:, cs])
        pieces.append(main_piece)
    for c0 in range(0, _C_END, MXU_COLS):
        def conv_piece(cs=slice(c0, c0 + MXU_COLS)):
            pc_ref[:, cs] = _dot(hbuf[...], wconv_ref[:, cs])
        pieces.append(conv_piece)

    def keep_halo():
        hbuf[0:HALO, :] = hbuf[ts:ts + HALO, :]
    pieces.append(keep_halo)
    return pieces


def _mixer_heads(pm, pcbuf, c1_ref, c2_ref, c1t_ref, c2t_ref, qn_ref, wqt_ref, kvn_ref, wk_ref,
                 wvt_ref, scw_ref, ssw_ref, ssb_ref, dtb_ref, aneg_ref, dskip_ref, snorm_ref,
                 qt_ref, k_ref, vt_ref, ycs_ref, ubuf, ybuf, state_ref, ts, pieces):
    def tick():
        if pieces:
            pieces.pop(0)()

    scale = MLA_QK_DIM ** -0.5 * math.log2(math.e)
    lane = lax.broadcasted_iota(jnp.int32, (1, LANES), 1)

    qn = _rms(pm[:, _M_CQ:_M_CQ + MLA_Q_LORA], qn_ref[...]).astype(_BF16)
    qt = _dot_nt(wqt_ref[...], qn)
    c1t = c1t_ref[...]
    c2t = c2t_ref[...]
    for hd in range(MLA_HEADS):
        a = qt[hd * HEAD_PAD:(hd + 1) * HEAD_PAD, :]
        a_swapped = jnp.concatenate([a[MLA_ROPE_DIM:, :], a[:MLA_ROPE_DIM, :]], axis=0)
        qt_ref[hd * HEAD_PAD:(hd + 1) * HEAD_PAD, :] = (
            (a * c1t + a_swapped * c2t) * scale).astype(_BF16)
        if hd == MLA_HEADS - 1:
            tick()
    kvn = _rms(pm[:, _M_CKV:_M_CKV + MLA_KV_LORA], kvn_ref[...]).astype(_BF16)
    vt_ref[...] = _dot_nt(wvt_ref[...], kvn).astype(_BF16)
    kfull = _dot(kvn, wk_ref[...])
    krdt = pm[:, _M_KRDT:_M_KRDT + LANES]
    kr = _rope_block(krdt, jnp.where(lane < MLA_NOPE_DIM, 0.0, c1_ref[...]), c2_ref[...])
    for hd in range(MLA_HEADS):
        sl = slice(hd * HEAD_PAD, (hd + 1) * HEAD_PAD)
        k_ref[:, sl] = (kfull[:, sl] + kr).astype(_BF16)

    ubuf[...] = pcbuf[:, _C_SCC:_C_SCC + SC_DIM] * pcbuf[:, _C_SCH:_C_SCH + SC_DIM]
    conv = _causal_taps(ubuf, scw_ref, slice(0, SC_DIM), slice(0, SC_DIM), ts)
    ycs_ref[:, 0:SC_DIM] = (pm[:, _M_SCB:_M_SCB + SC_DIM] * conv).astype(_BF16)

    xbc = _causal_taps(pcbuf, ssw_ref, slice(_C_XBC, _C_END), slice(0, SSD_CONV_DIM), ts)
    xbc = xbc + ssb_ref[...]
    xbc = xbc * jax.nn.sigmoid(xbc)
    z = pm[:, _M_Z:_M_Z + SSD_DIM]
    dt_raw = krdt + dtb_ref[...]
    dt = jnp.maximum(dt_raw, 0.0) + jnp.log1p(jnp.exp(-jnp.abs(dt_raw)))
    adt = dt * aneg_ref[...]
    tick()

    L = SSD_CHUNK
    nchunk = ts // L
    rows = lax.broadcasted_iota(jnp.int32, (L, L), 0)
    cols = lax.broadcasted_iota(jnp.int32, (L, L), 1)
    tri = rows >= cols
    head_lanes = lane < SSD_HEADS
    packed = jnp.where(head_lanes, adt[0:L, :], 0.0)
    for c in range(1, nchunk):
        packed = packed + pltpu.roll(jnp.where(head_lanes, adt[c * L:(c + 1) * L, :], 0.0),
                                     c * SSD_HEADS, 1)
    a_cs = _dot_f32(tri.astype(_F32), packed)
    a_cs_t = a_cs.T
    n = SSD_STATE
    first = lane < SSD_HEAD_DIM
    row2 = lax.broadcasted_iota(jnp.int32, (2 * n, LANES), 0)
    lane2 = lax.broadcasted_iota(jnp.int32, (2 * n, LANES), 1)
    diag_blocks = (row2 < n) == (lane2 < SSD_HEAD_DIM)
    states = [state_ref[g] for g in range(SSD_GROUPS)]
    for c in range(nchunk):
        r = slice(c * L, (c + 1) * L)
        dt_c = dt[r, :]
        for g in range(SSD_GROUPS):
            gsl = slice(g * LANES, (g + 1) * LANES)
            xs_g = xbc[r, gsl]
            bg = xbc[r, SSD_DIM + g * n:SSD_DIM + (g + 1) * n]
            cg = xbc[r, SSD_DIM + SSD_BC_DIM + g * n:SSD_DIM + SSD_BC_DIM + (g + 1) * n]
            sc = _dot_nt(cg.astype(_BF16), bg.astype(_BF16))
            tick()
            h0 = g * 2
            ln = c * SSD_HEADS + h0
            col = [a_cs[:, ln + k:ln + k + 1] for k in range(2)]
            row = [a_cs_t[ln + k:ln + k + 1, :] for k in range(2)]
            end = [a_cs[L - 1:L, ln + k:ln + k + 1] for k in range(2)]
            xdt = xs_g * jnp.where(first, dt_c[:, h0:h0 + 1], dt_c[:, h0 + 1:h0 + 2])
            xdt_blocks = jnp.concatenate(
                [jnp.where(first, xdt, 0.0).astype(_BF16), jnp.where(first, 0.0, xdt).astype(_BF16)],
                axis=0)
            dec = [jnp.exp(jnp.where(tri, col[k] - row[k], -jnp.inf)) for k in range(2)]
            m_g = jnp.concatenate([(sc * dec[k]).astype(_BF16) for k in range(2)], axis=1)
            y = _dot(m_g, xdt_blocks)
            cge = jnp.concatenate([(cg * jnp.exp(col[k])).astype(_BF16) for k in range(2)], axis=1)
            y = y + _dot(cge, states[g].astype(_BF16))
            bd = jnp.concatenate(
                [(bg * jnp.exp(end[k] - col[k])).astype(_BF16) for k in range(2)], axis=1)
            upd = _dot_tn(bd, xdt.astype(_BF16))
            decay = jnp.where(row2 < n, jnp.exp(end[0]), jnp.exp(end[1]))
            states[g] = states[g] * decay + jnp.where(diag_blocks, upd, 0.0)
            ybuf[r, gsl] = y + xs_g * dskip_ref[:, gsl]
    for g in range(SSD_GROUPS):
        state_ref[g] = states[g]
    while pieces:
        tick()
    yg = ybuf[...] * (z * jax.nn.sigmoid(z))
    ycs_ref[:, SC_DIM:SC_DIM + SSD_DIM] = _rms(yg, snorm_ref[...]).astype(_BF16)


def _mixer_in_kernel(x_ref, c1_ref, c2_ref, c1t_ref, c2t_ref, gpre_ref, wmain_ref, wconv_ref,
                     qn_ref, wqt_ref, kvn_ref, wk_ref, wvt_ref, scw_ref, ssw_ref, ssb_ref, dtb_ref,
                     aneg_ref, dskip_ref, snorm_ref,
                     qt_ref, k_ref, vt_ref, ycs_ref,
                     hbuf, pm0, pm1, pc0, pc1, ubuf, ybuf, state_ref, *, ts, nt):
    i = pl.program_id(0)

    @pl.when(i == 0)
    def _():
        pm1[...] = jnp.zeros_like(pm1)
        pc1[...] = jnp.zeros_like(pc1)
        state_ref[...] = jnp.zeros_like(state_ref)

    @pl.when(lax.rem(i, nt) == 0)
    def _():
        hbuf[0:HALO, :] = jnp.zeros((HALO, D_MODEL), _BF16)

    @pl.when(lax.rem(i + nt - 1, nt) == 0)
    def _():
        state_ref[...] = jnp.zeros_like(state_ref)

    def step(pm_w, pc_w, pm_r, pc_r):
        pieces = _mixer_project(x_ref, gpre_ref, wmain_ref, wconv_ref, hbuf, pm_w, pc_w, ts)
        _mixer_heads(pm_r, pc_r, c1_ref, c2_ref, c1t_ref, c2t_ref, qn_ref, wqt_ref, kvn_ref,
                     wk_ref, wvt_ref, scw_ref, ssw_ref, ssb_ref, dtb_ref, aneg_ref, dskip_ref,
                     snorm_ref, qt_ref, k_ref, vt_ref, ycs_ref, ubuf, ybuf, state_ref, ts, pieces)

    @pl.when(lax.rem(i, 2) == 0)
    def _():
        step(pm0, pc0, pm1, pc1)

    @pl.when(lax.rem(i, 2) == 1)
    def _():
        step(pm1, pc1, pm0, pc0)


def _mixer_in(x, rope, p, l, ts):
    b, s, d = x.shape
    nt = s // ts
    ntiles = b * nt
    t_in = lambda i: jnp.minimum(i, ntiles - 1)
    t_out = lambda i: jnp.maximum(i - 1, 0)
    tile = lambda w, t: pl.BlockSpec((None, ts, w), lambda i: (t(i) // nt, t(i) % nt, 0))
    tile_t = lambda w, t: pl.BlockSpec((None, w, ts), lambda i: (t(i) // nt, 0, t(i) % nt))
    layer = lambda a: pl.BlockSpec((None,) + a.shape[1:], lambda i: (l, 0, 0))
    c1, c2, c1t, c2t = rope
    weights = [p["gpre"], p["wmain"], p["wconv"], p["qn"], p["wqt"], p["kvn"], p["wk"], p["wvt"],
               p["scw"], p["ssw"], p["ssb"], p["dtb"], p["aneg"], p["dskip"], p["snorm"]]
    out_shape = [
        jax.ShapeDtypeStruct((b, MLA_HEADS * HEAD_PAD, s), _BF16),
        jax.ShapeDtypeStruct((b, s, MLA_HEADS * HEAD_PAD), _BF16),
        jax.ShapeDtypeStruct((b, MLA_OUT, s), _BF16),
        jax.ShapeDtypeStruct((b, s, SC_DIM + SSD_DIM), _BF16),
    ]
    return pl.pallas_call(
        functools.partial(_mixer_in_kernel, ts=ts, nt=nt),
        grid=(ntiles + 1,),
        in_specs=[tile(d, t_in), tile(LANES, t_out), tile(LANES, t_out),
                  tile_t(LANES, t_out), tile_t(LANES, t_out)]
        + [layer(w) for w in weights],
        out_specs=[tile_t(MLA_HEADS * HEAD_PAD, t_out), tile(MLA_HEADS * HEAD_PAD, t_out),
                   tile_t(MLA_OUT, t_out), tile(SC_DIM + SSD_DIM, t_out)],
        out_shape=out_shape,
        scratch_shapes=[
            pltpu.VMEM((ts + HALO, D_MODEL), _BF16),
            pltpu.VMEM((ts, _M_END), _F32), pltpu.VMEM((ts, _M_END), _F32),
            pltpu.VMEM((ts + HALO, _C_END), _F32), pltpu.VMEM((ts + HALO, _C_END), _F32),
            pltpu.VMEM((ts + HALO, SC_DIM), _F32),
            pltpu.VMEM((ts, SSD_DIM), _F32),
            pltpu.VMEM((SSD_GROUPS, 2 * SSD_STATE, LANES), _F32),
        ],
        compiler_params=pltpu.CompilerParams(
            dimension_semantics=("arbitrary",), vmem_limit_bytes=VMEM_LIMIT),
        name="mixer_in",
    )(x, c1, c2, c1t, c2t, *weights)


def _attention_kernel(qt_ref, k_ref, vt_ref, o_ref, s0_ref, s1_ref, m_ref, acc_ref, *, t):
    qi = pl.program_id(1)
    half = t // 2
    tri = (lax.broadcasted_iota(jnp.int32, (half, half), 0)
           <= lax.broadcasted_iota(jnp.int32, (half, half), 1))
    sbufs = (s0_ref, s1_ref)
    hsl = lambda hd: slice(hd * HEAD_PAD, (hd + 1) * HEAD_PAD)
    vsl = lambda hd: slice(hd * MLA_V_DIM, (hd + 1) * MLA_V_DIM)

    def online(st, vt, m, acc):
        m_new = jnp.maximum(m, jnp.max(st, axis=0, keepdims=True))
        alpha = jnp.exp2(m - m_new)
        pt = jnp.exp2(st - m_new).astype(_BF16)
        lhs = jnp.concatenate([vt, jnp.ones((ATT_SUM_ROWS, vt.shape[1]), _BF16)], axis=0)
        return m_new, alpha * acc + _dot(lhs, pt)

    def scores(hd, j):
        start = pl.multiple_of(j * t, t)
        sbufs[hd % 2][...] = _dot(k_ref[pl.ds(start, t), hsl(hd)], qt_ref[hsl(hd), :])

    def softmax_pv(hd, j):
        start = pl.multiple_of(j * t, t)
        m, acc = online(sbufs[hd % 2][...], vt_ref[vsl(hd), pl.ds(start, t)],
                        m_ref[hd], acc_ref[hd])
        m_ref[hd] = m
        acc_ref[hd] = acc

    def scores_diag(hd):
        lo = pl.multiple_of(qi * t, t)
        hi = pl.multiple_of(qi * t + half, half)
        sref = sbufs[hd % 2]
        sref[0:half, :] = _dot(k_ref[pl.ds(lo, half), hsl(hd)], qt_ref[hsl(hd), :])
        sref[half:, half:] = _dot(k_ref[pl.ds(hi, half), hsl(hd)], qt_ref[hsl(hd), half:])

    def softmax_pv_diag(hd):
        lo = pl.multiple_of(qi * t, t)
        hi = pl.multiple_of(qi * t + half, half)
        sref = sbufs[hd % 2]
        st = jnp.concatenate([jnp.where(tri, sref[0:half, 0:half], -jnp.inf),
                              sref[0:half, half:]], axis=1)
        m, acc = online(st, vt_ref[vsl(hd), pl.ds(lo, half)], m_ref[hd], acc_ref[hd])
        st = jnp.where(tri, sref[half:, half:], -jnp.inf)
        _, acc_hi = online(st, vt_ref[vsl(hd), pl.ds(hi, half)], m[:, half:], acc[:, half:])
        acc = jnp.concatenate([acc[:, :half], acc_hi], axis=1)
        o = acc[0:MLA_V_DIM, :] / acc[MLA_V_DIM:MLA_V_DIM + 1, :]
        o_ref[:, vsl(hd)] = o.T.astype(_BF16)

    m_ref[...] = jnp.full(m_ref.shape, -jnp.inf, _F32)
    acc_ref[...] = jnp.zeros(acc_ref.shape, _F32)

    scores(0, 0)

    def chunk_items(j):
        for hd in range(MLA_HEADS):
            if hd + 1 < MLA_HEADS:
                scores(hd + 1, j)
            else:
                scores(0, j + 1)
            softmax_pv(hd, j)

    def body(jj, carry):
        chunk_items(2 * jj)
        chunk_items(2 * jj + 1)
        return carry

    lax.fori_loop(0, lax.shift_right_logical(qi, 1), body, 0)

    @pl.when(lax.rem(qi, 2) == 1)
    def _():
        chunk_items(qi - 1)

    for hd in range(MLA_HEADS):
        if hd + 1 < MLA_HEADS:
            scores_diag(hd + 1)
        softmax_pv_diag(hd)


def _attention(qt, k, vt, t):
    b, s, _ = k.shape
    return pl.pallas_call(
        functools.partial(_attention_kernel, t=t),
        grid=(b, s // t),
        in_specs=[
            pl.BlockSpec((None, MLA_HEADS * HEAD_PAD, t), lambda bi, qi: (bi, 0, qi)),
            pl.BlockSpec((None, s, MLA_HEADS * HEAD_PAD), lambda bi, qi: (bi, 0, 0)),
            pl.BlockSpec((None, MLA_OUT, s), lambda bi, qi: (bi, 0, 0)),
        ],
        out_specs=pl.BlockSpec((None, t, MLA_OUT), lambda bi, qi: (bi, qi, 0)),
        out_shape=jax.ShapeDtypeStruct((b, s, MLA_OUT), _BF16),
        scratch_shapes=[
            pltpu.VMEM((t, t), _F32), pltpu.VMEM((t, t), _F32),
            pltpu.VMEM((MLA_HEADS, 1, t), _F32),
            pltpu.VMEM((MLA_HEADS, MLA_V_DIM + ATT_SUM_ROWS, t), _F32),
        ],
        compiler_params=pltpu.CompilerParams(
            dimension_semantics=("arbitrary", "arbitrary"), vmem_limit_bytes=VMEM_LIMIT),
        name="attention",
    )(qt, k, vt)


def _out_ffn_kernel(x_ref, att_ref, ycs_ref, woa_ref, woc_ref, gpost_ref, gfpre_ref, wg_ref,
                    wu_ref, cwg_ref, cwu_ref, cbg_ref, cbu_ref, wd_ref, gfpost_ref,
                    o_ref, hbuf, gbuf, ubuf, *, tm):
    si = pl.program_id(1)

    @pl.when(si == 0)
    def _():
        hbuf[0:HALO, :] = jnp.zeros((HALO, D_MODEL), _BF16)

    half = tm // 2
    x1s = []
    for k in range(2):
        rows = slice(k * half, (k + 1) * half)
        mixed = _dot(att_ref[rows, :], woa_ref[...]) + _dot(ycs_ref[rows, :], woc_ref[...])
        x1 = x_ref[rows, :] + _rms(mixed, gpost_ref[...])
        hbuf[HALO + k * half:HALO + (k + 1) * half, :] = _rms(x1, gfpre_ref[...]).astype(_BF16)
        x1s.append(x1)
    cols = slice(0, FFN_DIM)
    gbuf[...] = _dot(hbuf[...], wg_ref[...])
    ubuf[...] = _dot(hbuf[...], wu_ref[...])
    g = _causal_taps(gbuf, cwg_ref, cols, cols, tm) + cbg_ref[...]
    u = _causal_taps(ubuf, cwu_ref, cols, cols, tm) + cbu_ref[...]
    act = (g * jax.nn.sigmoid(g) * u).astype(_BF16)
    hbuf[0:HALO, :] = hbuf[tm:tm + HALO, :]
    for k in range(2):
        rows = slice(k * half, (k + 1) * half)
        y = _dot(act[rows, :], wd_ref[...])
        o_ref[rows, :] = x1s[k] + _rms(y, gfpost_ref[...])


def _out_ffn(x, att, ycs, p, l, tm):
    b, s, d = x.shape
    tile = lambda w: pl.BlockSpec((None, tm, w), lambda bi, si: (bi, si, 0))

    def layer(a, axis=None, part=0):
        shape, idx = list(a.shape[1:]), [0, 0]
        if axis is not None:
            shape[axis - 1] //= 2
            idx[axis - 1] = part
        return pl.BlockSpec((None, *shape), lambda bi, si: (l, *idx), pipeline_mode=pl.Buffered(1))

    operands = [
        (p["wo"], layer(p["wo"], 1, 0)), (p["wo"], layer(p["wo"], 1, 1)),
        (p["gpost"], layer(p["gpost"])), (p["gfpre"], layer(p["gfpre"])),
        (p["wup"], layer(p["wup"], 2, 0)), (p["wup"], layer(p["wup"], 2, 1)),
        (p["cw"], layer(p["cw"], 2, 0)), (p["cw"], layer(p["cw"], 2, 1)),
        (p["cb"], layer(p["cb"], 2, 0)), (p["cb"], layer(p["cb"], 2, 1)),
        (p["wd"], layer(p["wd"])), (p["gfpost"], layer(p["gfpost"])),
    ]
    weights = [a for a, _ in operands]
    return pl.pallas_call(
        functools.partial(_out_ffn_kernel, tm=tm),
        grid=(b, s // tm),
        in_specs=[tile(d), tile(MLA_OUT), tile(SC_DIM + SSD_DIM)] + [sp for _, sp in operands],
        out_specs=tile(d),
        out_shape=jax.ShapeDtypeStruct((b, s, d), _F32),
        scratch_shapes=[
            pltpu.VMEM((tm + HALO, D_MODEL), _BF16),
            pltpu.VMEM((tm + HALO, FFN_DIM), _F32),
            pltpu.VMEM((tm + HALO, FFN_DIM), _F32),
        ],
        compiler_params=pltpu.CompilerParams(
            dimension_semantics=("arbitrary", "arbitrary"), vmem_limit_bytes=VMEM_LIMIT),
        name="out_ffn",
    )(x, att, ycs, *weights)


def _swap_halves(w):
    half = w.shape[-1] // 2
    return jnp.concatenate([w[..., half:], w[..., :half]], axis=-1)


def _pack_params(norm_mix_pre, norm_mix_post, norm_ffn_pre, norm_ffn_post, w_in, mla_q_norm,
                 mla_w_q_up, mla_kv_norm, mla_w_kv_up, sc_conv_w, ssd_conv_w, ssd_conv_b,
                 ssd_dt_bias, ssd_a_log, ssd_d, ssd_norm, w_out, ffn_w_up, ffn_conv_w, ffn_conv_b,
                 ffn_w_down):
    depth = w_in.shape[0]
    f32 = lambda a: a.astype(_F32)
    row = lambda a: f32(a)[:, None, :]
    splits = np.cumsum(IN_WIDTHS)[:-1]
    w_cq, w_ckv, w_kr, w_scb, w_scc, w_sch, w_ssd = jnp.split(w_in, splits, axis=-1)
    w_z = w_ssd[..., :SSD_DIM]
    w_xbc = w_ssd[..., SSD_DIM:SSD_DIM + SSD_CONV_DIM]
    w_dt = w_ssd[..., SSD_DIM + SSD_CONV_DIM:]
    zeros = jnp.zeros((depth, D_MODEL, MLA_NOPE_DIM - SSD_HEADS), w_in.dtype)
    wmain = jnp.concatenate(
        [w_cq, w_ckv, w_dt, zeros, w_kr, _swap_halves(w_kr), w_scb, w_z], axis=-1)
    wconv = jnp.concatenate([w_scc, w_sch, w_xbc], axis=-1)

    wq4 = mla_w_q_up.reshape(depth, MLA_Q_LORA, MLA_HEADS, MLA_QK_DIM)
    wq = jnp.concatenate([wq4, _swap_halves(wq4[..., MLA_NOPE_DIM:])], axis=-1)
    wq = wq.reshape(depth, MLA_Q_LORA, MLA_HEADS * HEAD_PAD)
    wkv4 = mla_w_kv_up.reshape(depth, MLA_KV_LORA, MLA_HEADS, MLA_NOPE_DIM + MLA_V_DIM)
    wk = jnp.pad(wkv4[..., :MLA_NOPE_DIM], ((0, 0),) * 3 + ((0, HEAD_PAD - MLA_NOPE_DIM),))
    wk = wk.reshape(depth, MLA_KV_LORA, MLA_HEADS * HEAD_PAD)
    wvt = jnp.swapaxes(wkv4[..., MLA_NOPE_DIM:].reshape(depth, MLA_KV_LORA, MLA_OUT), 1, 2)

    pad_heads = lambda a: jnp.pad(f32(a), ((0, 0), (0, LANES - SSD_HEADS)))[:, None, :]
    return {
        "gpre": row(norm_mix_pre), "wmain": wmain.astype(_BF16), "wconv": wconv.astype(_BF16),
        "qn": row(mla_q_norm), "wqt": jnp.swapaxes(wq, 1, 2).astype(_BF16),
        "kvn": row(mla_kv_norm), "wk": wk.astype(_BF16), "wvt": wvt.astype(_BF16),
        "scw": f32(sc_conv_w), "ssw": f32(ssd_conv_w),
        "ssb": row(ssd_conv_b), "dtb": pad_heads(ssd_dt_bias),
        "aneg": -jnp.exp(pad_heads(ssd_a_log)),
        "dskip": jnp.repeat(f32(ssd_d), SSD_HEAD_DIM, axis=-1)[:, None, :],
        "snorm": row(ssd_norm),
        "wo": w_out.astype(_BF16), "gpost": row(norm_mix_post), "gfpre": row(norm_ffn_pre),
        "wup": ffn_w_up.astype(_BF16), "cw": f32(ffn_conv_w), "cb": row(ffn_conv_b),
        "wd": ffn_w_down.astype(_BF16), "gfpost": row(norm_ffn_post),
    }


def _rope_tables(positions):
    inv_freq = 1.0 / (ROPE_THETA ** (jnp.arange(0, MLA_ROPE_DIM, 2, dtype=_F32) / MLA_ROPE_DIM))
    ang = positions.astype(_F32)[..., None] * inv_freq
    cos = jnp.cos(ang)
    sin = jnp.sin(ang)
    lead = positions.shape + (MLA_NOPE_DIM,)
    tail = positions.shape + (LANES - MLA_NOPE_DIM - MLA_ROPE_DIM,)
    c1 = jnp.concatenate([jnp.ones(lead, _F32), cos, cos, jnp.zeros(tail, _F32)], axis=-1)
    c2 = jnp.concatenate([jnp.zeros(lead, _F32), -sin, sin, jnp.zeros(tail, _F32)], axis=-1)
    return c1, c2, jnp.swapaxes(c1, 1, 2), jnp.swapaxes(c2, 1, 2)


def kernel(x, positions, norm_mix_pre, norm_mix_post, norm_ffn_pre, norm_ffn_post, w_in, mla_q_norm, mla_w_q_up, mla_kv_norm, mla_w_kv_up, sc_conv_w, ssd_conv_w, ssd_conv_b, ssd_dt_bias, ssd_a_log, ssd_d, ssd_norm, w_out, ffn_w_up, ffn_conv_w, ffn_conv_b, ffn_w_down):
    depth = w_in.shape[0]
    s = x.shape[1]
    ts = min(SEQ_TILE_MIX, s)
    ta = min(SEQ_TILE_ATT, s)
    tm = min(SEQ_TILE_FFN, s)
    assert s % ts == 0 and s % ta == 0 and s % tm == 0 and ts % SSD_CHUNK == 0
    rope = _rope_tables(positions)
    p = _pack_params(norm_mix_pre, norm_mix_post, norm_ffn_pre, norm_ffn_post, w_in, mla_q_norm,
                     mla_w_q_up, mla_kv_norm, mla_w_kv_up, sc_conv_w, ssd_conv_w, ssd_conv_b,
                     ssd_dt_bias, ssd_a_log, ssd_d, ssd_norm, w_out, ffn_w_up, ffn_conv_w,
                     ffn_conv_b, ffn_w_down)
    for l in range(depth):
        qt, k, vt, ycs = _mixer_in(x, rope, p, l, ts)
        att = _attention(qt, k, vt, ta)
        x = _out_ffn(x, att, ycs, p, l, tm)
    return x
```

```python
import functools
import math

import jax
import jax.numpy as jnp
import numpy as np
from jax import lax
from jax.experimental import pallas as pl
from jax.experimental.pallas import tpu as pltpu

D_MODEL = 1024
MLA_HEADS = 8
MLA_Q_LORA = 256
MLA_KV_LORA = 128
MLA_NOPE_DIM = 64
MLA_ROPE_DIM = 32
MLA_V_DIM = 64
ROPE_THETA = 10000.0
SC_DIM = 256
SC_WIDTH = 3
SSD_HEADS = 4
SSD_HEAD_DIM = 64
SSD_GROUPS = 2
SSD_STATE = 128
SSD_CONV_WIDTH = 4
SSD_CHUNK = 128
FFN_DIM = 2816
FFN_CONV_WIDTH = 3
NORM_EPS = 1e-6

MLA_QK_DIM = MLA_NOPE_DIM + MLA_ROPE_DIM
MLA_OUT = MLA_HEADS * MLA_V_DIM
SSD_DIM = SSD_HEADS * SSD_HEAD_DIM
SSD_BC_DIM = SSD_GROUPS * SSD_STATE
SSD_CONV_DIM = SSD_DIM + 2 * SSD_BC_DIM
SSD_IN = SSD_DIM + SSD_CONV_DIM + SSD_HEADS
IN_WIDTHS = (MLA_Q_LORA, MLA_KV_LORA, MLA_ROPE_DIM, SC_DIM, SC_DIM, SC_DIM, SSD_IN)
D_IN = sum(IN_WIDTHS)

LANES = 128
MXU_COLS = 256
HEAD_PAD = LANES
HALO = 16

_M_CQ = 0
_M_CKV = _M_CQ + MLA_Q_LORA
_M_KRDT = _M_CKV + MLA_KV_LORA
_M_SCB = _M_KRDT + LANES
_M_Z = _M_SCB + SC_DIM
_M_END = _M_Z + SSD_DIM
_C_SCC = 0
_C_SCH = _C_SCC + SC_DIM
_C_XBC = _C_SCH + SC_DIM
_C_END = _C_XBC + SSD_CONV_DIM

SEQ_TILE_MIX = 512
SEQ_TILE_ATT = 512
ATT_SUM_ROWS = 16
SEQ_TILE_FFN = 512
VMEM_LIMIT = 56 * 1024 * 1024

assert _M_END % MXU_COLS == 0 and _C_END % MXU_COLS == 0
assert SSD_HEADS == 2 * SSD_GROUPS and 2 * SSD_HEAD_DIM == LANES
assert FFN_DIM % MXU_COLS == 0

_F32 = jnp.float32
_BF16 = jnp.bfloat16


def _rms(x, w):
    return x * lax.rsqrt(jnp.mean(x * x, axis=-1, keepdims=True) + NORM_EPS) * w


def _dot(a, b):
    return jnp.dot(a, b, preferred_element_type=_F32)


def _dot_nt(a, b):
    return lax.dot_general(a, b, (((1,), (1,)), ((), ())), preferred_element_type=_F32)


def _dot_tn(a, b):
    return lax.dot_general(a, b, (((0,), (0,)), ((), ())), preferred_element_type=_F32)


def _dot_f32(a, b):
    return jnp.dot(a, b, preferred_element_type=_F32, precision=lax.Precision.HIGHEST)


def _causal_taps(buf, w_ref, cols, wcols, rows):
    width = w_ref.shape[0]
    out = buf[HALO:HALO + rows, cols] * w_ref[width - 1:width, wcols]
    for i in range(width - 1):
        o = HALO - width + 1 + i
        out = out + buf[o:o + rows, cols] * w_ref[i:i + 1, wcols]
    return out


def _rope_block(a, c1, c2):
    return a * c1 + pltpu.roll(a, LANES - MLA_ROPE_DIM, 1) * c2


def _mixer_project(x_ref, gpre_ref, wmain_ref, wconv_ref, hbuf, pm_ref, pc_ref, ts):
    hbuf[HALO:, :] = _rms(x_ref[...], gpre_ref[...]).astype(_BF16)
    pieces = []
    for c0 in range(0, _M_END, MXU_COLS):
        def main_piece(cs=slice(c0, c0 + MXU_COLS)):
            pm_ref[:, cs] = _dot(hbuf[HALO:, :], wmain_ref[:, cs])
        pieces.append(main_piece)
    for c0 in range(0, _C_END, MXU_COLS):
        def conv_piece(cs=slice(c0, c0 + MXU_COLS)):
            pc_ref[:, cs] = _dot(hbuf[...], wconv_ref[:, cs])
        pieces.append(conv_piece)

    def keep_halo():
        hbuf[0:HALO, :] = hbuf[ts:ts + HALO, :]
    pieces.append(keep_halo)
    return pieces


def _mixer_heads(pm, pcbuf, c1_ref, c2_ref, c1t_ref, c2t_ref, qn_ref, wqt_ref, kvn_ref, wk_ref,
                 wvt_ref, scw_ref, ssw_ref, ssb_ref, dtb_ref, aneg_ref, dskip_ref, snorm_ref,
                 qt_ref, k_ref, vt_ref, ycs_ref, ubuf, ybuf, state_ref, ts, pieces):
    def tick():
        if pieces:
            pieces.pop(0)()

    scale = MLA_QK_DIM ** -0.5 * math.log2(math.e)
    lane = lax.broadcasted_iota(jnp.int32, (1, LANES), 1)

    qn = _rms(pm[:, _M_CQ:_M_CQ + MLA_Q_LORA], qn_ref[...]).astype(_BF16)
    qt = _dot_nt(wqt_ref[...], qn)
    c1t = c1t_ref[...]
    c2t = c2t_ref[...]
    for hd in range(MLA_HEADS):
        a = qt[hd * HEAD_PAD:(hd + 1) * HEAD_PAD, :]
        a_swapped = jnp.concatenate([a[MLA_ROPE_DIM:, :], a[:MLA_ROPE_DIM, :]], axis=0)
        qt_ref[hd * HEAD_PAD:(hd + 1) * HEAD_PAD, :] = (
            (a * c1t + a_swapped * c2t) * scale).astype(_BF16)
        if hd == MLA_HEADS - 1:
            tick()
    kvn = _rms(pm[:, _M_CKV:_M_CKV + MLA_KV_LORA], kvn_ref[...]).astype(_BF16)
    vt_ref[...] = _dot_nt(wvt_ref[...], kvn).astype(_BF16)
    kfull = _dot(kvn, wk_ref[...])
    krdt = pm[:, _M_KRDT:_M_KRDT + LANES]
    kr = _rope_block(krdt, jnp.where(lane < MLA_NOPE_DIM, 0.0, c1_ref[...]), c2_ref[...])
    for hd in range(MLA_HEADS):
        sl = slice(hd * HEAD_PAD, (hd + 1) * HEAD_PAD)
        k_ref[hd] = (kfull[:, sl] + kr).astype(_BF16)

    ubuf[...] = pcbuf[:, _C_SCC:_C_SCC + SC_DIM] * pcbuf[:, _C_SCH:_C_SCH + SC_DIM]
    conv = _causal_taps(ubuf, scw_ref, slice(0, SC_DIM), slice(0, SC_DIM), ts)
    ycs_ref[:, 0:SC_DIM] = (pm[:, _M_SCB:_M_SCB + SC_DIM] * conv).astype(_BF16)

    xbc = _causal_taps(pcbuf, ssw_ref, slice(_C_XBC, _C_END), slice(0, SSD_CONV_DIM), ts)
    xbc = xbc + ssb_ref[...]
    xbc = xbc * jax.nn.sigmoid(xbc)
    z = pm[:, _M_Z:_M_Z + SSD_DIM]
    dt_raw = krdt + dtb_ref[...]
    dt = jnp.maximum(dt_raw, 0.0) + jnp.log1p(jnp.exp(-jnp.abs(dt_raw)))
    adt = dt * aneg_ref[...]
    tick()

    L = SSD_CHUNK
    nchunk = ts // L
    rows = lax.broadcasted_iota(jnp.int32, (L, L), 0)
    cols = lax.broadcasted_iota(jnp.int32, (L, L), 1)
    tri = rows >= cols
    head_lanes = lane < SSD_HEADS
    packed = jnp.where(head_lanes, adt[0:L, :], 0.0)
    for c in range(1, nchunk):
        packed = packed + pltpu.roll(jnp.where(head_lanes, adt[c * L:(c + 1) * L, :], 0.0),
                                     c * SSD_HEADS, 1)
    a_cs = _dot_f32(tri.astype(_F32), packed)
    a_cs_t = a_cs.T
    n = SSD_STATE
    first = lane < SSD_HEAD_DIM
    row2 = lax.broadcasted_iota(jnp.int32, (2 * n, LANES), 0)
    lane2 = lax.broadcasted_iota(jnp.int32, (2 * n, LANES), 1)
    diag_blocks = (row2 < n) == (lane2 < SSD_HEAD_DIM)
    states = [state_ref[g] for g in range(SSD_GROUPS)]
    for c in range(nchunk):
        r = slice(c * L, (c + 1) * L)
        dt_c = dt[r, :]
        for g in range(SSD_GROUPS):
            gsl = slice(g * LANES, (g + 1) * LANES)
            xs_g = xbc[r, gsl]
            bg = xbc[r, SSD_DIM + g * n:SSD_DIM + (g + 1) * n]
            cg = xbc[r, SSD_DIM + SSD_BC_DIM + g * n:SSD_DIM + SSD_BC_DIM + (g + 1) * n]
            sc = _dot_nt(cg.astype(_BF16), bg.astype(_BF16))
            tick()
            h0 = g * 2
            ln = c * SSD_HEADS + h0
            col = [a_cs[:, ln + k:ln + k + 1] for k in range(2)]
            row = [a_cs_t[ln + k:ln + k + 1, :] for k in range(2)]
            end = [a_cs[L - 1:L, ln + k:ln + k + 1] for k in range(2)]
            xdt = xs_g * jnp.where(first, dt_c[:, h0:h0 + 1], dt_c[:, h0 + 1:h0 + 2])
            xdt_blocks = jnp.concatenate(
                [jnp.where(first, xdt, 0.0).astype(_BF16), jnp.where(first, 0.0, xdt).astype(_BF16)],
                axis=0)
            dec = [jnp.exp(jnp.where(tri, col[k] - row[k], -jnp.inf)) for k in range(2)]
            m_g = jnp.concatenate([(sc * dec[k]).astype(_BF16) for k in range(2)], axis=1)
            y = _dot(m_g, xdt_blocks)
            cge = jnp.concatenate([(cg * jnp.exp(col[k])).astype(_BF16) for k in range(2)], axis=1)
            y = y + _dot(cge, states[g].astype(_BF16))
            bd = jnp.concatenate(
                [(bg * jnp.exp(end[k] - col[k])).astype(_BF16) for k in range(2)], axis=1)
            upd = _dot_tn(bd, xdt.astype(_BF16))
            decay = jnp.where(row2 < n, jnp.exp(end[0]), jnp.exp(end[1]))
            states[g] = states[g] * decay + jnp.where(diag_blocks, upd, 0.0)
            ybuf[r, gsl] = y + xs_g * dskip_ref[:, gsl]
    for g in range(SSD_GROUPS):
        state_ref[g] = states[g]
    while pieces:
        tick()
    yg = ybuf[...] * (z * jax.nn.sigmoid(z))
    ycs_ref[:, SC_DIM:SC_DIM + SSD_DIM] = _rms(yg, snorm_ref[...]).astype(_BF16)


def _mixer_in_kernel(x_ref, c1_ref, c2_ref, c1t_ref, c2t_ref, gpre_ref, wmain_ref, wconv_ref,
                     qn_ref, wqt_ref, kvn_ref, wk_ref, wvt_ref, scw_ref, ssw_ref, ssb_ref, dtb_ref,
                     aneg_ref, dskip_ref, snorm_ref,
                     qt_ref, k_ref, vt_ref, ycs_ref,
                     hbuf, pm0, pm1, pc0, pc1, ubuf, ybuf, state_ref, *, ts, nt):
    i = pl.program_id(0)

    @pl.when(i == 0)
    def _():
        pm1[...] = jnp.zeros_like(pm1)
        pc1[...] = jnp.zeros_like(pc1)
        state_ref[...] = jnp.zeros_like(state_ref)

    @pl.when(lax.rem(i, nt) == 0)
    def _():
        hbuf[0:HALO, :] = jnp.zeros((HALO, D_MODEL), _BF16)

    @pl.when(lax.rem(i + nt - 1, nt) == 0)
    def _():
        state_ref[...] = jnp.zeros_like(state_ref)

    def step(pm_w, pc_w, pm_r, pc_r):
        pieces = _mixer_project(x_ref, gpre_ref, wmain_ref, wconv_ref, hbuf, pm_w, pc_w, ts)
        _mixer_heads(pm_r, pc_r, c1_ref, c2_ref, c1t_ref, c2t_ref, qn_ref, wqt_ref, kvn_ref,
                     wk_ref, wvt_ref, scw_ref, ssw_ref, ssb_ref, dtb_ref, aneg_ref, dskip_ref,
                     snorm_ref, qt_ref, k_ref, vt_ref, ycs_ref, ubuf, ybuf, state_ref, ts, pieces)

    @pl.when(lax.rem(i, 2) == 0)
    def _():
        step(pm0, pc0, pm1, pc1)

    @pl.when(lax.rem(i, 2) == 1)
    def _():
        step(pm1, pc1, pm0, pc0)


def _mixer_in(x, rope, p, l, ts):
    b, s, d = x.shape
    nt = s // ts
    ntiles = b * nt
    t_in = lambda i: jnp.minimum(i, ntiles - 1)
    t_out = lambda i: jnp.maximum(i - 1, 0)
    tile = lambda w, t: pl.BlockSpec((None, ts, w), lambda i: (t(i) // nt, t(i) % nt, 0))
    tile_t = lambda w, t: pl.BlockSpec((None, w, ts), lambda i: (t(i) // nt, 0, t(i) % nt))
    layer = lambda a: pl.BlockSpec((None,) + a.shape[1:], lambda i: (l, 0, 0))
    c1, c2, c1t, c2t = rope
    weights = [p["gpre"], p["wmain"], p["wconv"], p["qn"], p["wqt"], p["kvn"], p["wk"], p["wvt"],
               p["scw"], p["ssw"], p["ssb"], p["dtb"], p["aneg"], p["dskip"], p["snorm"]]
    out_shape = [
        jax.ShapeDtypeStruct((b, MLA_HEADS * HEAD_PAD, s), _BF16),
        jax.ShapeDtypeStruct((b, MLA_HEADS, s, HEAD_PAD), _BF16),
        jax.ShapeDtypeStruct((b, nt, MLA_OUT, ts), _BF16),
        jax.ShapeDtypeStruct((b, s, SC_DIM + SSD_DIM), _BF16),
    ]
    return pl.pallas_call(
        functools.partial(_mixer_in_kernel, ts=ts, nt=nt),
        grid=(ntiles + 1,),
        in_specs=[tile(d, t_in), tile(LANES, t_out), tile(LANES, t_out),
                  tile_t(LANES, t_out), tile_t(LANES, t_out)]
        + [layer(w) for w in weights],
        out_specs=[tile_t(MLA_HEADS * HEAD_PAD, t_out),
                   pl.BlockSpec((None, MLA_HEADS, ts, HEAD_PAD),
                                lambda i: (t_out(i) // nt, 0, t_out(i) % nt, 0)),
                   pl.BlockSpec((None, None, MLA_OUT, ts),
                                lambda i: (t_out(i) // nt, t_out(i) % nt, 0, 0)),
                   tile(SC_DIM + SSD_DIM, t_out)],
        out_shape=out_shape,
        scratch_shapes=[
            pltpu.VMEM((ts + HALO, D_MODEL), _BF16),
            pltpu.VMEM((ts, _M_END), _F32), pltpu.VMEM((ts, _M_END), _F32),
            pltpu.VMEM((ts + HALO, _C_END), _F32), pltpu.VMEM((ts + HALO, _C_END), _F32),
            pltpu.VMEM((ts + HALO, SC_DIM), _F32),
            pltpu.VMEM((ts, SSD_DIM), _F32),
            pltpu.VMEM((SSD_GROUPS, 2 * SSD_STATE, LANES), _F32),
        ],
        compiler_params=pltpu.CompilerParams(
            dimension_semantics=("arbitrary",), vmem_limit_bytes=VMEM_LIMIT),
        name="mixer_in",
    )(x, c1, c2, c1t, c2t, *weights)


def _attention_kernel(qt_ref, k_ref, vt_ref, o_ref, s0_ref, s1_ref, m_ref, acc_ref, *, t):
    qi = pl.program_id(1)
    half = t // 2
    tri = (lax.broadcasted_iota(jnp.int32, (half, half), 0)
           <= lax.broadcasted_iota(jnp.int32, (half, half), 1))
    sbufs = (s0_ref, s1_ref)
    hsl = lambda hd: slice(hd * HEAD_PAD, (hd + 1) * HEAD_PAD)
    vsl = lambda hd: slice(hd * MLA_V_DIM, (hd + 1) * MLA_V_DIM)

    def online(st, vt, m, acc):
        m_new = jnp.maximum(m, jnp.max(st, axis=0, keepdims=True))
        alpha = jnp.exp2(m - m_new)
        pt = jnp.exp2(st - m_new).astype(_BF16)
        lhs = jnp.concatenate([vt, jnp.ones((ATT_SUM_ROWS, vt.shape[1]), _BF16)], axis=0)
        return m_new, alpha * acc + _dot(lhs, pt)

    def scores(hd, j):
        start = pl.multiple_of(j * t, t)
        sbufs[hd % 2][...] = _dot(k_ref[hd, pl.ds(start, t), :], qt_ref[hsl(hd), :])

    def softmax_pv(hd, j):
        start = pl.multiple_of(j * t, t)
        m, acc = online(sbufs[hd % 2][...], vt_ref[j, vsl(hd), :],
                        m_ref[hd], acc_ref[hd])
        m_ref[hd] = m
        acc_ref[hd] = acc

    def scores_diag(hd):
        lo = pl.multiple_of(qi * t, t)
        hi = pl.multiple_of(qi * t + half, half)
        sref = sbufs[hd % 2]
        sref[0:half, :] = _dot(k_ref[hd, pl.ds(lo, half), :], qt_ref[hsl(hd), :])
        sref[half:, half:] = _dot(k_ref[hd, pl.ds(hi, half), :], qt_ref[hsl(hd), half:])

    def softmax_pv_diag(hd):
        lo = pl.multiple_of(qi * t, t)
        hi = pl.multiple_of(qi * t + half, half)
        sref = sbufs[hd % 2]
        st = jnp.concatenate([jnp.where(tri, sref[0:half, 0:half], -jnp.inf),
                              sref[0:half, half:]], axis=1)
        m, acc = online(st, vt_ref[qi, vsl(hd), 0:half], m_ref[hd], acc_ref[hd])
        st = jnp.where(tri, sref[half:, half:], -jnp.inf)
        _, acc_hi = online(st, vt_ref[qi, vsl(hd), half:], m[:, half:], acc[:, half:])
        acc = jnp.concatenate([acc[:, :half], acc_hi], axis=1)
        o = acc[0:MLA_V_DIM, :] / acc[MLA_V_DIM:MLA_V_DIM + 1, :]
        o_ref[:, vsl(hd)] = o.T.astype(_BF16)

    m_ref[...] = jnp.full(m_ref.shape, -jnp.inf, _F32)
    acc_ref[...] = jnp.zeros(acc_ref.shape, _F32)

    scores(0, 0)

    def chunk_items(j):
        for hd in range(MLA_HEADS):
            if hd + 1 < MLA_HEADS:
                scores(hd + 1, j)
            else:
                scores(0, j + 1)
            softmax_pv(hd, j)

    def body(jj, carry):
        chunk_items(2 * jj)
        chunk_items(2 * jj + 1)
        return carry

    lax.fori_loop(0, lax.shift_right_logical(qi, 1), body, 0)

    @pl.when(lax.rem(qi, 2) == 1)
    def _():
        chunk_items(qi - 1)

    for hd in range(MLA_HEADS):
        if hd + 1 < MLA_HEADS:
            scores_diag(hd + 1)
        softmax_pv_diag(hd)


def _attention(qt, k, vt, t):
    b, _, s, _ = k.shape
    return pl.pallas_call(
        functools.partial(_attention_kernel, t=t),
        grid=(b, s // t),
        in_specs=[
            pl.BlockSpec((None, MLA_HEADS * HEAD_PAD, t), lambda bi, qi: (bi, 0, qi)),
            pl.BlockSpec((None, MLA_HEADS, s, HEAD_PAD), lambda bi, qi: (bi, 0, 0, 0)),
            pl.BlockSpec((None, s // t, MLA_OUT, t), lambda bi, qi: (bi, 0, 0, 0)),
        ],
        out_specs=pl.BlockSpec((None, t, MLA_OUT), lambda bi, qi: (bi, qi, 0)),
        out_shape=jax.ShapeDtypeStruct((b, s, MLA_OUT), _BF16),
        scratch_shapes=[
            pltpu.VMEM((t, t), _F32), pltpu.VMEM((t, t), _F32),
            pltpu.VMEM((MLA_HEADS, 1, t), _F32),
            pltpu.VMEM((MLA_HEADS, MLA_V_DIM + ATT_SUM_ROWS, t), _F32),
        ],
        compiler_params=pltpu.CompilerParams(
            dimension_semantics=("arbitrary", "arbitrary"), vmem_limit_bytes=VMEM_LIMIT),
        name="attention",
    )(qt, k, vt)


def _out_ffn_kernel(x_ref, att_ref, ycs_ref, woa_ref, woc_ref, gpost_ref, gfpre_ref, wg_ref,
                    wu_ref, cwg_ref, cwu_ref, cbg_ref, cbu_ref, wd_ref, gfpost_ref,
                    o_ref, hbuf, gbuf, ubuf, *, tm):
    si = pl.program_id(1)

    @pl.when(si == 0)
    def _():
        hbuf[0:HALO, :] = jnp.zeros((HALO, D_MODEL), _BF16)

    half = tm // 2
    x1s = []
    for k in range(2):
        rows = slice(k * half, (k + 1) * half)
        mixed = _dot(att_ref[rows, :], woa_ref[...]) + _dot(ycs_ref[rows, :], woc_ref[...])
        x1 = x_ref[rows, :] + _rms(mixed, gpost_ref[...])
        hbuf[HALO + k * half:HALO + (k + 1) * half, :] = _rms(x1, gfpre_ref[...]).astype(_BF16)
        x1s.append(x1)
    cols = slice(0, FFN_DIM)
    gbuf[...] = _dot(hbuf[...], wg_ref[...])
    ubuf[...] = _dot(hbuf[...], wu_ref[...])
    g = _causal_taps(gbuf, cwg_ref, cols, cols, tm) + cbg_ref[...]
    u = _causal_taps(ubuf, cwu_ref, cols, cols, tm) + cbu_ref[...]
    act = (g * jax.nn.sigmoid(g) * u).astype(_BF16)
    hbuf[0:HALO, :] = hbuf[tm:tm + HALO, :]
    for k in range(2):
        rows = slice(k * half, (k + 1) * half)
        y = _dot(act[rows, :], wd_ref[...])
        o_ref[rows, :] = x1s[k] + _rms(y, gfpost_ref[...])


def _out_ffn(x, att, ycs, p, l, tm):
    b, s, d = x.shape
    tile = lambda w: pl.BlockSpec((None, tm, w), lambda bi, si: (bi, si, 0))

    def layer(a, axis=None, part=0):
        shape, idx = list(a.shape[1:]), [0, 0]
        if axis is not None:
            shape[axis - 1] //= 2
            idx[axis - 1] = part
        return pl.BlockSpec((None, *shape), lambda bi, si: (l, *idx), pipeline_mode=pl.Buffered(1))

    operands = [
        (p["wo"], layer(p["wo"], 1, 0)), (p["wo"], layer(p["wo"], 1, 1)),
        (p["gpost"], layer(p["gpost"])), (p["gfpre"], layer(p["gfpre"])),
        (p["wup"], layer(p["wup"], 2, 0)), (p["wup"], layer(p["wup"], 2, 1)),
        (p["cw"], layer(p["cw"], 2, 0)), (p["cw"], layer(p["cw"], 2, 1)),
        (p["cb"], layer(p["cb"], 2, 0)), (p["cb"], layer(p["cb"], 2, 1)),
        (p["wd"], layer(p["wd"])), (p["gfpost"], layer(p["gfpost"])),
    ]
    weights = [a for a, _ in operands]
    return pl.pallas_call(
        functools.partial(_out_ffn_kernel, tm=tm),
        grid=(b, s // tm),
        in_specs=[tile(d), tile(MLA_OUT), tile(SC_DIM + SSD_DIM)] + [sp for _, sp in operands],
        out_specs=tile(d),
        out_shape=jax.ShapeDtypeStruct((b, s, d), _F32),
        scratch_shapes=[
            pltpu.VMEM((tm + HALO, D_MODEL), _BF16),
            pltpu.VMEM((tm + HALO, FFN_DIM), _F32),
            pltpu.VMEM((tm + HALO, FFN_DIM), _F32),
        ],
        compiler_params=pltpu.CompilerParams(
            dimension_semantics=("arbitrary", "arbitrary"), vmem_limit_bytes=VMEM_LIMIT),
        name="out_ffn",
    )(x, att, ycs, *weights)


def _swap_halves(w):
    half = w.shape[-1] // 2
    return jnp.concatenate([w[..., half:], w[..., :half]], axis=-1)


def _pack_params(norm_mix_pre, norm_mix_post, norm_ffn_pre, norm_ffn_post, w_in, mla_q_norm,
                 mla_w_q_up, mla_kv_norm, mla_w_kv_up, sc_conv_w, ssd_conv_w, ssd_conv_b,
                 ssd_dt_bias, ssd_a_log, ssd_d, ssd_norm, w_out, ffn_w_up, ffn_conv_w, ffn_conv_b,
                 ffn_w_down):
    depth = w_in.shape[0]
    f32 = lambda a: a.astype(_F32)
    row = lambda a: f32(a)[:, None, :]
    splits = np.cumsum(IN_WIDTHS)[:-1]
    w_cq, w_ckv, w_kr, w_scb, w_scc, w_sch, w_ssd = jnp.split(w_in, splits, axis=-1)
    w_z = w_ssd[..., :SSD_DIM]
    w_xbc = w_ssd[..., SSD_DIM:SSD_DIM + SSD_CONV_DIM]
    w_dt = w_ssd[..., SSD_DIM + SSD_CONV_DIM:]
    zeros = jnp.zeros((depth, D_MODEL, MLA_NOPE_DIM - SSD_HEADS), w_in.dtype)
    wmain = jnp.concatenate(
        [w_cq, w_ckv, w_dt, zeros, w_kr, _swap_halves(w_kr), w_scb, w_z], axis=-1)
    wconv = jnp.concatenate([w_scc, w_sch, w_xbc], axis=-1)

    wq4 = mla_w_q_up.reshape(depth, MLA_Q_LORA, MLA_HEADS, MLA_QK_DIM)
    wq = jnp.concatenate([wq4, _swap_halves(wq4[..., MLA_NOPE_DIM:])], axis=-1)
    wq = wq.reshape(depth, MLA_Q_LORA, MLA_HEADS * HEAD_PAD)
    wkv4 = mla_w_kv_up.reshape(depth, MLA_KV_LORA, MLA_HEADS, MLA_NOPE_DIM + MLA_V_DIM)
    wk = jnp.pad(wkv4[..., :MLA_NOPE_DIM], ((0, 0),) * 3 + ((0, HEAD_PAD - MLA_NOPE_DIM),))
    wk = wk.reshape(depth, MLA_KV_LORA, MLA_HEADS * HEAD_PAD)
    wvt = jnp.swapaxes(wkv4[..., MLA_NOPE_DIM:].reshape(depth, MLA_KV_LORA, MLA_OUT), 1, 2)

    pad_heads = lambda a: jnp.pad(f32(a), ((0, 0), (0, LANES - SSD_HEADS)))[:, None, :]
    return {
        "gpre": row(norm_mix_pre), "wmain": wmain.astype(_BF16), "wconv": wconv.astype(_BF16),
        "qn": row(mla_q_norm), "wqt": jnp.swapaxes(wq, 1, 2).astype(_BF16),
        "kvn": row(mla_kv_norm), "wk": wk.astype(_BF16), "wvt": wvt.astype(_BF16),
        "scw": f32(sc_conv_w), "ssw": f32(ssd_conv_w),
        "ssb": row(ssd_conv_b), "dtb": pad_heads(ssd_dt_bias),
        "aneg": -jnp.exp(pad_heads(ssd_a_log)),
        "dskip": jnp.repeat(f32(ssd_d), SSD_HEAD_DIM, axis=-1)[:, None, :],
        "snorm": row(ssd_norm),
        "wo": w_out.astype(_BF16), "gpost": row(norm_mix_post), "gfpre": row(norm_ffn_pre),
        "wup": ffn_w_up.astype(_BF16), "cw": f32(ffn_conv_w), "cb": row(ffn_conv_b),
        "wd": ffn_w_down.astype(_BF16), "gfpost": row(norm_ffn_post),
    }


def _rope_tables(positions):
    inv_freq = 1.0 / (ROPE_THETA ** (jnp.arange(0, MLA_ROPE_DIM, 2, dtype=_F32) / MLA_ROPE_DIM))
    ang = positions.astype(_F32)[..., None] * inv_freq
    cos = jnp.cos(ang)
    sin = jnp.sin(ang)
    lead = positions.shape + (MLA_NOPE_DIM,)
    tail = positions.shape + (LANES - MLA_NOPE_DIM - MLA_ROPE_DIM,)
    c1 = jnp.concatenate([jnp.ones(lead, _F32), cos, cos, jnp.zeros(tail, _F32)], axis=-1)
    c2 = jnp.concatenate([jnp.zeros(lead, _F32), -sin, sin, jnp.zeros(tail, _F32)], axis=-1)
    return c1, c2, jnp.swapaxes(c1, 1, 2), jnp.swapaxes(c2, 1, 2)


def kernel(x, positions, norm_mix_pre, norm_mix_post, norm_ffn_pre, norm_ffn_post, w_in, mla_q_norm, mla_w_q_up, mla_kv_norm, mla_w_kv_up, sc_conv_w, ssd_conv_w, ssd_conv_b, ssd_dt_bias, ssd_a_log, ssd_d, ssd_norm, w_out, ffn_w_up, ffn_conv_w, ffn_conv_b, ffn_w_down):
    depth = w_in.shape[0]
    s = x.shape[1]
    ts = min(SEQ_TILE_MIX, s)
    ta = min(SEQ_TILE_ATT, s)
    tm = min(SEQ_TILE_FFN, s)
    assert ts == ta and s % ts == 0 and s % tm == 0 and ts % SSD_CHUNK == 0
    rope = _rope_tables(positions)
    p = _pack_params(norm_mix_pre, norm_mix_post, norm_ffn_pre, norm_ffn_post, w_in, mla_q_norm,
                     mla_w_q_up, mla_kv_norm, mla_w_kv_up, sc_conv_w, ssd_conv_w, ssd_conv_b,
                     ssd_dt_bias, ssd_a_log, ssd_d, ssd_norm, w_out, ffn_w_up, ffn_conv_w,
                     ffn_conv_b, ffn_w_down)
    for l in range(depth):
        qt, k, vt, ycs = _mixer_in(x, rope, p, l, ts)
        att = _attention(qt, k, vt, ta)
        x = _out_ffn(x, att, ycs, p, l, tm)
    return x
```

```python
import functools
import math

import jax
import jax.numpy as jnp
import numpy as np
from jax import lax
from jax.experimental import pallas as pl
from jax.experimental.pallas import tpu as pltpu

D_MODEL = 1024
MLA_HEADS = 8
MLA_Q_LORA = 256
MLA_KV_LORA = 128
MLA_NOPE_DIM = 64
MLA_ROPE_DIM = 32
MLA_V_DIM = 64
ROPE_THETA = 10000.0
SC_DIM = 256
SC_WIDTH = 3
SSD_HEADS = 4
SSD_HEAD_DIM = 64
SSD_GROUPS = 2
SSD_STATE = 128
SSD_CONV_WIDTH = 4
SSD_CHUNK = 128
FFN_DIM = 2816
FFN_CONV_WIDTH = 3
NORM_EPS = 1e-6

MLA_QK_DIM = MLA_NOPE_DIM + MLA_ROPE_DIM
MLA_OUT = MLA_HEADS * MLA_V_DIM
SSD_DIM = SSD_HEADS * SSD_HEAD_DIM
SSD_BC_DIM = SSD_GROUPS * SSD_STATE
SSD_CONV_DIM = SSD_DIM + 2 * SSD_BC_DIM
SSD_IN = SSD_DIM + SSD_CONV_DIM + SSD_HEADS
IN_WIDTHS = (MLA_Q_LORA, MLA_KV_LORA, MLA_ROPE_DIM, SC_DIM, SC_DIM, SC_DIM, SSD_IN)
D_IN = sum(IN_WIDTHS)

LANES = 128
MXU_COLS = 256
HEAD_PAD = LANES
HALO = 16

_M_CQ = 0
_M_CKV = _M_CQ + MLA_Q_LORA
_M_KRDT = _M_CKV + MLA_KV_LORA
_M_SCB = _M_KRDT + LANES
_M_Z = _M_SCB + SC_DIM
_M_END = _M_Z + SSD_DIM
_C_SCC = 0
_C_SCH = _C_SCC + SC_DIM
_C_XBC = _C_SCH + SC_DIM
_C_END = _C_XBC + SSD_CONV_DIM

SEQ_TILE_MIX = 512
SEQ_TILE_ATT = 512
ATT_SUM_ROWS = 16
SEQ_TILE_FFN = 512
VMEM_LIMIT = 56 * 1024 * 1024

assert _M_END % MXU_COLS == 0 and _C_END % MXU_COLS == 0
assert SSD_HEADS == 2 * SSD_GROUPS and 2 * SSD_HEAD_DIM == LANES
assert FFN_DIM % MXU_COLS == 0

_F32 = jnp.float32
_BF16 = jnp.bfloat16


def _rms(x, w):
    return x * lax.rsqrt(jnp.mean(x * x, axis=-1, keepdims=True) + NORM_EPS) * w


def _dot(a, b):
    return jnp.dot(a, b, preferred_element_type=_F32)


def _dot_nt(a, b):
    return lax.dot_general(a, b, (((1,), (1,)), ((), ())), preferred_element_type=_F32)


def _dot_tn(a, b):
    return lax.dot_general(a, b, (((0,), (0,)), ((), ())), preferred_element_type=_F32)


def _dot_f32(a, b):
    return jnp.dot(a, b, preferred_element_type=_F32, precision=lax.Precision.HIGHEST)


def _causal_taps(buf, w_ref, cols, wcols, rows):
    width = w_ref.shape[0]
    out = buf[HALO:HALO + rows, cols] * w_ref[width - 1:width, wcols]
    for i in range(width - 1):
        o = HALO - width + 1 + i
        out = out + buf[o:o + rows, cols] * w_ref[i:i + 1, wcols]
    return out


def _rope_block(a, c1, c2):
    return a * c1 + pltpu.roll(a, LANES - MLA_ROPE_DIM, 1) * c2


def _mixer_project(x_ref, gpre_ref, wmain_ref, wconv_ref, hbuf, pm_ref, pc_ref, ts):
    hbuf[HALO:, :] = _rms(x_ref[...], gpre_ref[...]).astype(_BF16)
    pieces = []
    for c0 in range(0, _M_END, MXU_COLS):
        def main_piece(cs=slice(c0, c0 + MXU_COLS)):
            pm_ref[:, cs] = _dot(hbuf[HALO:, :], wmain_ref[:, cs])
        pieces.append(main_piece)
    for c0 in range(0, _C_END, MXU_COLS):
        def conv_piece(cs=slice(c0, c0 + MXU_COLS)):
            pc_ref[:, cs] = _dot(hbuf[...], wconv_ref[:, cs])
        pieces.append(conv_piece)

    def keep_halo():
        hbuf[0:HALO, :] = hbuf[ts:ts + HALO, :]
    pieces.append(keep_halo)
    return pieces


def _mixer_heads(pm, pcbuf, c1_ref, c2_ref, c1t_ref, c2t_ref, qn_ref, wqt_ref, kvn_ref, wk_ref,
                 wvt_ref, scw_ref, ssw_ref, ssb_ref, dtb_ref, aneg_ref, dskip_ref, snorm_ref,
                 qt_ref, k_ref, vt_ref, ycs_ref, ubuf, ybuf, state_ref, ts, pieces):
    def tick():
        if pieces:
            pieces.pop(0)()

    scale = MLA_QK_DIM ** -0.5 * math.log2(math.e)
    lane = lax.broadcasted_iota(jnp.int32, (1, LANES), 1)

    qn = _rms(pm[:, _M_CQ:_M_CQ + MLA_Q_LORA], qn_ref[...]).astype(_BF16)
    qt = _dot_nt(wqt_ref[...], qn)
    c1t = c1t_ref[...]
    c2t = c2t_ref[...]
    for hd in range(MLA_HEADS):
        a = qt[hd * HEAD_PAD:(hd + 1) * HEAD_PAD, :]
        a_swapped = jnp.concatenate([a[MLA_ROPE_DIM:, :], a[:MLA_ROPE_DIM, :]], axis=0)
        qt_ref[hd * HEAD_PAD:(hd + 1) * HEAD_PAD, :] = (
            (a * c1t + a_swapped * c2t) * scale).astype(_BF16)
        if hd == MLA_HEADS - 1:
            tick()
    kvn = _rms(pm[:, _M_CKV:_M_CKV + MLA_KV_LORA], kvn_ref[...]).astype(_BF16)
    vt_ref[...] = _dot_nt(wvt_ref[...], kvn).astype(_BF16)
    kfull = _dot(kvn, wk_ref[...])
    krdt = pm[:, _M_KRDT:_M_KRDT + LANES]
    kr = _rope_block(krdt, jnp.where(lane < MLA_NOPE_DIM, 0.0, c1_ref[...]), c2_ref[...])
    for hd in range(MLA_HEADS):
        sl = slice(hd * HEAD_PAD, (hd + 1) * HEAD_PAD)
        k_ref[:, sl] = (kfull[:, sl] + kr).astype(_BF16)

    ubuf[...] = pcbuf[:, _C_SCC:_C_SCC + SC_DIM] * pcbuf[:, _C_SCH:_C_SCH + SC_DIM]
    conv = _causal_taps(ubuf, scw_ref, slice(0, SC_DIM), slice(0, SC_DIM), ts)
    ycs_ref[:, 0:SC_DIM] = (pm[:, _M_SCB:_M_SCB + SC_DIM] * conv).astype(_BF16)

    xbc = _causal_taps(pcbuf, ssw_ref, slice(_C_XBC, _C_END), slice(0, SSD_CONV_DIM), ts)
    xbc = xbc + ssb_ref[...]
    xbc = xbc * jax.nn.sigmoid(xbc)
    z = pm[:, _M_Z:_M_Z + SSD_DIM]
    dt_raw = krdt + dtb_ref[...]
    dt = jnp.maximum(dt_raw, 0.0) + jnp.log1p(jnp.exp(-jnp.abs(dt_raw)))
    adt = dt * aneg_ref[...]
    tick()

    L = SSD_CHUNK
    nchunk = ts // L
    rows = lax.broadcasted_iota(jnp.int32, (L, L), 0)
    cols = lax.broadcasted_iota(jnp.int32, (L, L), 1)
    tri = rows >= cols
    head_lanes = lane < SSD_HEADS
    packed = jnp.where(head_lanes, adt[0:L, :], 0.0)
    for c in range(1, nchunk):
        packed = packed + pltpu.roll(jnp.where(head_lanes, adt[c * L:(c + 1) * L, :], 0.0),
                                     c * SSD_HEADS, 1)
    a_cs = _dot_f32(tri.astype(_F32), packed)
    a_cs_t = a_cs.T
    n = SSD_STATE
    first = lane < SSD_HEAD_DIM
    row2 = lax.broadcasted_iota(jnp.int32, (2 * n, LANES), 0)
    lane2 = lax.broadcasted_iota(jnp.int32, (2 * n, LANES), 1)
    diag_blocks = (row2 < n) == (lane2 < SSD_HEAD_DIM)
    states = [state_ref[g] for g in range(SSD_GROUPS)]
    for c in range(nchunk):
        r = slice(c * L, (c + 1) * L)
        dt_c = dt[r, :]
        for g in range(SSD_GROUPS):
            gsl = slice(g * LANES, (g + 1) * LANES)
            xs_g = xbc[r, gsl]
            bg = xbc[r, SSD_DIM + g * n:SSD_DIM + (g + 1) * n]
            cg = xbc[r, SSD_DIM + SSD_BC_DIM + g * n:SSD_DIM + SSD_BC_DIM + (g + 1) * n]
            sc = _dot_nt(cg.astype(_BF16), bg.astype(_BF16))
            tick()
            h0 = g * 2
            ln = c * SSD_HEADS + h0
            col = [a_cs[:, ln + k:ln + k + 1] for k in range(2)]
            row = [a_cs_t[ln + k:ln + k + 1, :] for k in range(2)]
            end = [a_cs[L - 1:L, ln + k:ln + k + 1] for k in range(2)]
            xdt = xs_g * jnp.where(first, dt_c[:, h0:h0 + 1], dt_c[:, h0 + 1:h0 + 2])
            xdt_blocks = jnp.concatenate(
                [jnp.where(first, xdt, 0.0).astype(_BF16), jnp.where(first, 0.0, xdt).astype(_BF16)],
                axis=0)
            dec = [jnp.exp(jnp.where(tri, col[k] - row[k], -jnp.inf)) for k in range(2)]
            m_g = jnp.concatenate([(sc * dec[k]).astype(_BF16) for k in range(2)], axis=1)
            y = _dot(m_g, xdt_blocks)
            cge = jnp.concatenate([(cg * jnp.exp(col[k])).astype(_BF16) for k in range(2)], axis=1)
            y = y + _dot(cge, states[g].astype(_BF16))
            bd = jnp.concatenate(
                [(bg * jnp.exp(end[k] - col[k])).astype(_BF16) for k in range(2)], axis=1)
            upd = _dot_tn(bd, xdt.astype(_BF16))
            decay = jnp.where(row2 < n, jnp.exp(end[0]), jnp.exp(end[1]))
            states[g] = states[g] * decay + jnp.where(diag_blocks, upd, 0.0)
            ybuf[r, gsl] = y + xs_g * dskip_ref[:, gsl]
    for g in range(SSD_GROUPS):
        state_ref[g] = states[g]
    while pieces:
        tick()
    yg = ybuf[...] * (z * jax.nn.sigmoid(z))
    ycs_ref[:, SC_DIM:SC_DIM + SSD_DIM] = _rms(yg, snorm_ref[...]).astype(_BF16)


def _mixer_in_kernel(x_ref, c1_ref, c2_ref, c1t_ref, c2t_ref, gpre_ref, wmain_ref, wconv_ref,
                     qn_ref, wqt_ref, kvn_ref, wk_ref, wvt_ref, scw_ref, ssw_ref, ssb_ref, dtb_ref,
                     aneg_ref, dskip_ref, snorm_ref,
                     qt_ref, k_ref, vt_ref, ycs_ref,
                     hbuf, pm0, pm1, pc0, pc1, ubuf, ybuf, state_ref, *, ts, nt):
    i = pl.program_id(0)

    @pl.when(i == 0)
    def _():
        pm1[...] = jnp.zeros_like(pm1)
        pc1[...] = jnp.zeros_like(pc1)
        state_ref[...] = jnp.zeros_like(state_ref)

    @pl.when(lax.rem(i, nt) == 0)
    def _():
        hbuf[0:HALO, :] = jnp.zeros((HALO, D_MODEL), _BF16)

    @pl.when(lax.rem(i + nt - 1, nt) == 0)
    def _():
        state_ref[...] = jnp.zeros_like(state_ref)

    def step(pm_w, pc_w, pm_r, pc_r):
        pieces = _mixer_project(x_ref, gpre_ref, wmain_ref, wconv_ref, hbuf, pm_w, pc_w, ts)
        _mixer_heads(pm_r, pc_r, c1_ref, c2_ref, c1t_ref, c2t_ref, qn_ref, wqt_ref, kvn_ref,
                     wk_ref, wvt_ref, scw_ref, ssw_ref, ssb_ref, dtb_ref, aneg_ref, dskip_ref,
                     snorm_ref, qt_ref, k_ref, vt_ref, ycs_ref, ubuf, ybuf, state_ref, ts, pieces)

    @pl.when(lax.rem(i, 2) == 0)
    def _():
        step(pm0, pc0, pm1, pc1)

    @pl.when(lax.rem(i, 2) == 1)
    def _():
        step(pm1, pc1, pm0, pc0)


def _mixer_in(x, rope, p, l, ts):
    b, s, d = x.shape
    nt = s // ts
    ntiles = b * nt
    t_in = lambda i: jnp.minimum(i, ntiles - 1)
    t_out = lambda i: jnp.maximum(i - 1, 0)
    tile = lambda w, t: pl.BlockSpec((None, ts, w), lambda i: (t(i) // nt, t(i) % nt, 0))
    tile_t = lambda w, t: pl.BlockSpec((None, w, ts), lambda i: (t(i) // nt, 0, t(i) % nt))
    layer = lambda a: pl.BlockSpec((None,) + a.shape[1:], lambda i: (l, 0, 0))
    c1, c2, c1t, c2t = rope
    weights = [p["gpre"], p["wmain"], p["wconv"], p["qn"], p["wqt"], p["kvn"], p["wk"], p["wvt"],
               p["scw"], p["ssw"], p["ssb"], p["dtb"], p["aneg"], p["dskip"], p["snorm"]]
    out_shape = [
        jax.ShapeDtypeStruct((b, MLA_HEADS * HEAD_PAD, s), _BF16),
        jax.ShapeDtypeStruct((b, s, MLA_HEADS * HEAD_PAD), _BF16),
        jax.ShapeDtypeStruct((b, MLA_OUT, s), _BF16),
        jax.ShapeDtypeStruct((b, s, SC_DIM + SSD_DIM), _BF16),
    ]
    return pl.pallas_call(
        functools.partial(_mixer_in_kernel, ts=ts, nt=nt),
        grid=(ntiles + 1,),
        in_specs=[tile(d, t_in), tile(LANES, t_out), tile(LANES, t_out),
                  tile_t(LANES, t_out), tile_t(LANES, t_out)]
        + [layer(w) for w in weights],
        out_specs=[tile_t(MLA_HEADS * HEAD_PAD, t_out), tile(MLA_HEADS * HEAD_PAD, t_out),
                   tile_t(MLA_OUT, t_out), tile(SC_DIM + SSD_DIM, t_out)],
        out_shape=out_shape,
        scratch_shapes=[
            pltpu.VMEM((ts + HALO, D_MODEL), _BF16),
            pltpu.VMEM((ts, _M_END), _F32), pltpu.VMEM((ts, _M_END), _F32),
            pltpu.VMEM((ts + HALO, _C_END), _F32), pltpu.VMEM((ts + HALO, _C_END), _F32),
            pltpu.VMEM((ts + HALO, SC_DIM), _F32),
            pltpu.VMEM((ts, SSD_DIM), _F32),
            pltpu.VMEM((SSD_GROUPS, 2 * SSD_STATE, LANES), _F32),
        ],
        compiler_params=pltpu.CompilerParams(
            dimension_semantics=("arbitrary",), vmem_limit_bytes=VMEM_LIMIT),
        name="mixer_in",
    )(x, c1, c2, c1t, c2t, *weights)


def _attention_kernel(qt_ref, k_ref, vt_ref, o_ref, s0_ref, s1_ref, m_ref, acc_ref, *, t):
    qi = pl.program_id(1)
    half = t // 2
    tri = (lax.broadcasted_iota(jnp.int32, (half, half), 0)
           <= lax.broadcasted_iota(jnp.int32, (half, half), 1))
    sbufs = (s0_ref, s1_ref)
    hsl = lambda hd: slice(hd * HEAD_PAD, (hd + 1) * HEAD_PAD)
    vsl = lambda hd: slice(hd * MLA_V_DIM, (hd + 1) * MLA_V_DIM)

    def online(st, vt, m, acc):
        m_new = jnp.maximum(m, jnp.max(st, axis=0, keepdims=True))
        alpha = jnp.exp2(m - m_new)
        pt = jnp.exp2(st - m_new).astype(_BF16)
        lhs = jnp.concatenate([vt, jnp.ones((ATT_SUM_ROWS, vt.shape[1]), _BF16)], axis=0)
        return m_new, alpha * acc + _dot(lhs, pt)

    def scores(hd, j):
        start = pl.multiple_of(j * t, t)
        sbufs[hd % 2][...] = _dot(k_ref[pl.ds(start, t), hsl(hd)], qt_ref[hsl(hd), :])

    def softmax_pv(hd, j):
        start = pl.multiple_of(j * t, t)
        m, acc = online(sbufs[hd % 2][...], vt_ref[vsl(hd), pl.ds(start, t)],
                        m_ref[hd], acc_ref[hd])
        m_ref[hd] = m
        acc_ref[hd] = acc

    def scores_diag(hd):
        lo = pl.multiple_of(qi * t, t)
        hi = pl.multiple_of(qi * t + half, half)
        sref = sbufs[hd % 2]
        sref[0:half, :] = _dot(k_ref[pl.ds(lo, half), hsl(hd)], qt_ref[hsl(hd), :])
        sref[half:, half:] = _dot(k_ref[pl.ds(hi, half), hsl(hd)], qt_ref[hsl(hd), half:])

    def softmax_pv_diag(hd):
        lo = pl.multiple_of(qi * t, t)
        hi = pl.multiple_of(qi * t + half, half)
        sref = sbufs[hd % 2]
        st = jnp.concatenate([jnp.where(tri, sref[0:half, 0:half], -jnp.inf),
                              sref[0:half, half:]], axis=1)
        m, acc = online(st, vt_ref[vsl(hd), pl.ds(lo, half)], m_ref[hd], acc_ref[hd])
        st = jnp.where(tri, sref[half:, half:], -jnp.inf)
        _, acc_hi = online(st, vt_ref[vsl(hd), pl.ds(hi, half)], m[:, half:], acc[:, half:])
        acc = jnp.concatenate([acc[:, :half], acc_hi], axis=1)
        o = acc[0:MLA_V_DIM, :] / acc[MLA_V_DIM:MLA_V_DIM + 1, :]
        o_ref[:, vsl(hd)] = o.T.astype(_BF16)

    m_ref[...] = jnp.full(m_ref.shape, -jnp.inf, _F32)
    acc_ref[...] = jnp.zeros(acc_ref.shape, _F32)

    scores(0, 0)

    def chunk_items(j):
        for hd in range(MLA_HEADS):
            if hd + 1 < MLA_HEADS:
                scores(hd + 1, j)
            else:
                scores(0, j + 1)
            softmax_pv(hd, j)

    def body(jj, carry):
        for k in range(4):
            chunk_items(4 * jj + k)
        return carry

    lax.fori_loop(0, lax.shift_right_logical(qi, 2), body, 0)
    base = lax.shift_left(lax.shift_right_logical(qi, 2), 2)

    @pl.when(lax.rem(qi, 4) >= 2)
    def _():
        chunk_items(base)
        chunk_items(base + 1)

    @pl.when(lax.rem(qi, 2) == 1)
    def _():
        chunk_items(qi - 1)

    for hd in range(MLA_HEADS):
        if hd + 1 < MLA_HEADS:
            scores_diag(hd + 1)
        softmax_pv_diag(hd)


def _attention(qt, k, vt, t):
    b, s, _ = k.shape
    return pl.pallas_call(
        functools.partial(_attention_kernel, t=t),
        grid=(b, s // t),
        in_specs=[
            pl.BlockSpec((None, MLA_HEADS * HEAD_PAD, t), lambda bi, qi: (bi, 0, qi)),
            pl.BlockSpec((None, s, MLA_HEADS * HEAD_PAD), lambda bi, qi: (bi, 0, 0)),
            pl.BlockSpec((None, MLA_OUT, s), lambda bi, qi: (bi, 0, 0)),
        ],
        out_specs=pl.BlockSpec((None, t, MLA_OUT), lambda bi, qi: (bi, qi, 0)),
        out_shape=jax.ShapeDtypeStruct((b, s, MLA_OUT), _BF16),
        scratch_shapes=[
            pltpu.VMEM((t, t), _F32), pltpu.VMEM((t, t), _F32),
            pltpu.VMEM((MLA_HEADS, 1, t), _F32),
            pltpu.VMEM((MLA_HEADS, MLA_V_DIM + ATT_SUM_ROWS, t), _F32),
        ],
        compiler_params=pltpu.CompilerParams(
            dimension_semantics=("arbitrary", "arbitrary"), vmem_limit_bytes=VMEM_LIMIT),
        name="attention",
    )(qt, k, vt)


def _out_ffn_kernel(x_ref, att_ref, ycs_ref, woa_ref, woc_ref, gpost_ref, gfpre_ref, wg_ref,
                    wu_ref, cwg_ref, cwu_ref, cbg_ref, cbu_ref, wd_ref, gfpost_ref,
                    o_ref, hbuf, gbuf, ubuf, *, tm):
    si = pl.program_id(1)

    @pl.when(si == 0)
    def _():
        hbuf[0:HALO, :] = jnp.zeros((HALO, D_MODEL), _BF16)

    half = tm // 2
    x1s = []
    for k in range(2):
        rows = slice(k * half, (k + 1) * half)
        mixed = _dot(att_ref[rows, :], woa_ref[...]) + _dot(ycs_ref[rows, :], woc_ref[...])
        x1 = x_ref[rows, :] + _rms(mixed, gpost_ref[...])
        hbuf[HALO + k * half:HALO + (k + 1) * half, :] = _rms(x1, gfpre_ref[...]).astype(_BF16)
        x1s.append(x1)
    cols = slice(0, FFN_DIM)
    cut = HALO + half
    gbuf[0:cut, :] = _dot(hbuf[0:cut, :], wg_ref[...])
    ubuf[0:cut, :] = _dot(hbuf[0:cut, :], wu_ref[...])
    gbuf[cut:, :] = _dot(hbuf[cut:, :], wg_ref[...])
    ubuf[cut:, :] = _dot(hbuf[cut:, :], wu_ref[...])
    g = _causal_taps(gbuf, cwg_ref, cols, cols, tm) + cbg_ref[...]
    u = _causal_taps(ubuf, cwu_ref, cols, cols, tm) + cbu_ref[...]
    act = (g * jax.nn.sigmoid(g) * u).astype(_BF16)
    hbuf[0:HALO, :] = hbuf[tm:tm + HALO, :]
    for k in range(2):
        rows = slice(k * half, (k + 1) * half)
        y = _dot(act[rows, :], wd_ref[...])
        o_ref[rows, :] = x1s[k] + _rms(y, gfpost_ref[...])


def _out_ffn(x, att, ycs, p, l, tm):
    b, s, d = x.shape
    tile = lambda w: pl.BlockSpec((None, tm, w), lambda bi, si: (bi, si, 0))

    def layer(a, axis=None, part=0):
        shape, idx = list(a.shape[1:]), [0, 0]
        if axis is not None:
            shape[axis - 1] //= 2
            idx[axis - 1] = part
        return pl.BlockSpec((None, *shape), lambda bi, si: (l, *idx), pipeline_mode=pl.Buffered(1))

    operands = [
        (p["wo"], layer(p["wo"], 1, 0)), (p["wo"], layer(p["wo"], 1, 1)),
        (p["gpost"], layer(p["gpost"])), (p["gfpre"], layer(p["gfpre"])),
        (p["wup"], layer(p["wup"], 2, 0)), (p["wup"], layer(p["wup"], 2, 1)),
        (p["cw"], layer(p["cw"], 2, 0)), (p["cw"], layer(p["cw"], 2, 1)),
        (p["cb"], layer(p["cb"], 2, 0)), (p["cb"], layer(p["cb"], 2, 1)),
        (p["wd"], layer(p["wd"])), (p["gfpost"], layer(p["gfpost"])),
    ]
    weights = [a for a, _ in operands]
    return pl.pallas_call(
        functools.partial(_out_ffn_kernel, tm=tm),
        grid=(b, s // tm),
        in_specs=[tile(d), tile(MLA_OUT), tile(SC_DIM + SSD_DIM)] + [sp for _, sp in operands],
        out_specs=tile(d),
        out_shape=jax.ShapeDtypeStruct((b, s, d), _F32),
        scratch_shapes=[
            pltpu.VMEM((tm + HALO, D_MODEL), _BF16),
            pltpu.VMEM((tm + HALO, FFN_DIM), _F32),
            pltpu.VMEM((tm + HALO, FFN_DIM), _F32),
        ],
        compiler_params=pltpu.CompilerParams(
            dimension_semantics=("arbitrary", "arbitrary"), vmem_limit_bytes=VMEM_LIMIT),
        name="out_ffn",
    )(x, att, ycs, *weights)


def _swap_halves(w):
    half = w.shape[-1] // 2
    return jnp.concatenate([w[..., half:], w[..., :half]], axis=-1)


def _pack_params(norm_mix_pre, norm_mix_post, norm_ffn_pre, norm_ffn_post, w_in, mla_q_norm,
                 mla_w_q_up, mla_kv_norm, mla_w_kv_up, sc_conv_w, ssd_conv_w, ssd_conv_b,
                 ssd_dt_bias, ssd_a_log, ssd_d, ssd_norm, w_out, ffn_w_up, ffn_conv_w, ffn_conv_b,
                 ffn_w_down):
    depth = w_in.shape[0]
    f32 = lambda a: a.astype(_F32)
    row = lambda a: f32(a)[:, None, :]
    splits = np.cumsum(IN_WIDTHS)[:-1]
    w_cq, w_ckv, w_kr, w_scb, w_scc, w_sch, w_ssd = jnp.split(w_in, splits, axis=-1)
    w_z = w_ssd[..., :SSD_DIM]
    w_xbc = w_ssd[..., SSD_DIM:SSD_DIM + SSD_CONV_DIM]
    w_dt = w_ssd[..., SSD_DIM + SSD_CONV_DIM:]
    zeros = jnp.zeros((depth, D_MODEL, MLA_NOPE_DIM - SSD_HEADS), w_in.dtype)
    wmain = jnp.concatenate(
        [w_cq, w_ckv, w_dt, zeros, w_kr, _swap_halves(w_kr), w_scb, w_z], axis=-1)
    wconv = jnp.concatenate([w_scc, w_sch, w_xbc], axis=-1)

    wq4 = mla_w_q_up.reshape(depth, MLA_Q_LORA, MLA_HEADS, MLA_QK_DIM)
    wq = jnp.concatenate([wq4, _swap_halves(wq4[..., MLA_NOPE_DIM:])], axis=-1)
    wq = wq.reshape(depth, MLA_Q_LORA, MLA_HEADS * HEAD_PAD)
    wkv4 = mla_w_kv_up.reshape(depth, MLA_KV_LORA, MLA_HEADS, MLA_NOPE_DIM + MLA_V_DIM)
    wk = jnp.pad(wkv4[..., :MLA_NOPE_DIM], ((0, 0),) * 3 + ((0, HEAD_PAD - MLA_NOPE_DIM),))
    wk = wk.reshape(depth, MLA_KV_LORA, MLA_HEADS * HEAD_PAD)
    wvt = jnp.swapaxes(wkv4[..., MLA_NOPE_DIM:].reshape(depth, MLA_KV_LORA, MLA_OUT), 1, 2)

    pad_heads = lambda a: jnp.pad(f32(a), ((0, 0), (0, LANES - SSD_HEADS)))[:, None, :]
    return {
        "gpre": row(norm_mix_pre), "wmain": wmain.astype(_BF16), "wconv": wconv.astype(_BF16),
        "qn": row(mla_q_norm), "wqt": jnp.swapaxes(wq, 1, 2).astype(_BF16),
        "kvn": row(mla_kv_norm), "wk": wk.astype(_BF16), "wvt": wvt.astype(_BF16),
        "scw": f32(sc_conv_w), "ssw": f32(ssd_conv_w),
        "ssb": row(ssd_conv_b), "dtb": pad_heads(ssd_dt_bias),
        "aneg": -jnp.exp(pad_heads(ssd_a_log)),
        "dskip": jnp.repeat(f32(ssd_d), SSD_HEAD_DIM, axis=-1)[:, None, :],
        "snorm": row(ssd_norm),
        "wo": w_out.astype(_BF16), "gpost": row(norm_mix_post), "gfpre": row(norm_ffn_pre),
        "wup": ffn_w_up.astype(_BF16), "cw": f32(ffn_conv_w), "cb": row(ffn_conv_b),
        "wd": ffn_w_down.astype(_BF16), "gfpost": row(norm_ffn_post),
    }


def _rope_tables(positions):
    inv_freq = 1.0 / (ROPE_THETA ** (jnp.arange(0, MLA_ROPE_DIM, 2, dtype=_F32) / MLA_ROPE_DIM))
    ang = positions.astype(_F32)[..., None] * inv_freq
    cos = jnp.cos(ang)
    sin = jnp.sin(ang)
    lead = positions.shape + (MLA_NOPE_DIM,)
    tail = positions.shape + (LANES - MLA_NOPE_DIM - MLA_ROPE_DIM,)
    c1 = jnp.concatenate([jnp.ones(lead, _F32), cos, cos, jnp.zeros(tail, _F32)], axis=-1)
    c2 = jnp.concatenate([jnp.zeros(lead, _F32), -sin, sin, jnp.zeros(tail, _F32)], axis=-1)
    return c1, c2, jnp.swapaxes(c1, 1, 2), jnp.swapaxes(c2, 1, 2)


def kernel(x, positions, norm_mix_pre, norm_mix_post, norm_ffn_pre, norm_ffn_post, w_in, mla_q_norm, mla_w_q_up, mla_kv_norm, mla_w_kv_up, sc_conv_w, ssd_conv_w, ssd_conv_b, ssd_dt_bias, ssd_a_log, ssd_d, ssd_norm, w_out, ffn_w_up, ffn_conv_w, ffn_conv_b, ffn_w_down):
    depth = w_in.shape[0]
    s = x.shape[1]
    ts = min(SEQ_TILE_MIX, s)
    ta = min(SEQ_TILE_ATT, s)
    tm = min(SEQ_TILE_FFN, s)
    assert s % ts == 0 and s % ta == 0 and s % tm == 0 and ts % SSD_CHUNK == 0
    rope = _rope_tables(positions)
    p = _pack_params(norm_mix_pre, norm_mix_post, norm_ffn_pre, norm_ffn_post, w_in, mla_q_norm,
                     mla_w_q_up, mla_kv_norm, mla_w_kv_up, sc_conv_w, ssd_conv_w, ssd_conv_b,
                     ssd_dt_bias, ssd_a_log, ssd_d, ssd_norm, w_out, ffn_w_up, ffn_conv_w,
                     ffn_conv_b, ffn_w_down)
    for l in range(depth):
        qt, k, vt, ycs = _mixer_in(x, rope, p, l, ts)
        att = _attention(qt, k, vt, ta)
        x = _out_ffn(x, att, ycs, p, l, tm)
    return x
```
